```python
import jax, jax.numpy as jnp
from jax import lax
import numpy as np

D_MODEL = 2048
BATCH = 2
SEQ = 4096
DEPTH = 4
DEC_BATCH = 8
DEC_SEQ = 1
PAST_LEN = 16384
PAGE_SIZE = 128

CONV_WIDTH = 512
CONV_K = 31
LRU_WIDTH = 512
LRU_BLOCKS = 4
LRU_BLOCK = LRU_WIDTH // LRU_BLOCKS
LRU_CONV_K = 4
LRU_C = 8.0
SB_HEADS = 8
HEAD_DIM = 128
SB_WIDTH = SB_HEADS * HEAD_DIM
Q_BLOCK = 128
SB_BIAS_INIT = -8.0
D_MIX = CONV_WIDTH + LRU_WIDTH + SB_WIDTH
D_IN = 2 * CONV_WIDTH + 2 * LRU_WIDTH + 3 * SB_WIDTH
D_FF = -(-(8 * D_MODEL) // (3 * 256)) * 256
EPS = 1e-6
SPLITS = [CONV_WIDTH, 2 * CONV_WIDTH, 2 * CONV_WIDTH + LRU_WIDTH, 2 * CONV_WIDTH + 2 * LRU_WIDTH,
          2 * CONV_WIDTH + 2 * LRU_WIDTH + SB_WIDTH, 2 * CONV_WIDTH + 2 * LRU_WIDTH + 2 * SB_WIDTH]

kernel_name = 'hymba_conv_rglru_stickbreak_decoder_step'


def rmsnorm(x, g):
    xf = x.astype(jnp.float32)
    y = xf * lax.rsqrt(jnp.mean(xf * xf, axis=-1, keepdims=True) + EPS)
    return (y * g.astype(jnp.float32)).astype(x.dtype)


def layernorm(x, g, b):
    xf = x.astype(jnp.float32)
    mu = jnp.mean(xf, axis=-1, keepdims=True)
    xc = xf - mu
    var = jnp.mean(xc * xc, axis=-1, keepdims=True)
    return (xc * lax.rsqrt(var + EPS) * g.astype(jnp.float32) + b.astype(jnp.float32)).astype(x.dtype)


def causal_dwconv(x, buf, w, b):
    width = w.shape[0]
    xp = jnp.concatenate([buf.astype(x.dtype), x], axis=1)
    y = lax.conv_general_dilated(xp, w[:, None, :].astype(x.dtype), window_strides=(1,), padding='VALID',
                                 dimension_numbers=('NWC', 'WIO', 'NWC'), feature_group_count=x.shape[-1])
    return y + b.astype(x.dtype), xp[:, xp.shape[1] - (width - 1):]


def rglru(x, h0, wa, ba, wx, bx, lam):
    B, T, C = x.shape
    xf = x.astype(jnp.float32)
    xb = xf.reshape(B, T, LRU_BLOCKS, LRU_BLOCK)
    r = jax.nn.sigmoid(jnp.einsum('btnc,ncd->btnd', xb, wa.astype(jnp.float32)).reshape(B, T, C) + ba.astype(jnp.float32))
    i = jax.nn.sigmoid(jnp.einsum('btnc,ncd->btnd', xb, wx.astype(jnp.float32)).reshape(B, T, C) + bx.astype(jnp.float32))
    log_a = -LRU_C * r * jax.nn.softplus(-lam.astype(jnp.float32))
    a = jnp.exp(log_a)
    u = jnp.sqrt(-jnp.expm1(2.0 * log_a)) * (i * xf)
    u = jnp.concatenate([u[:, :1] + a[:, :1] * h0.astype(jnp.float32)[:, None, :], u[:, 1:]], axis=1)

    def combine(e1, e2):
        a1, b1 = e1
        a2, b2 = e2
        return a1 * a2, a2 * b1 + b2

    _, h = lax.associative_scan(combine, (a, u), axis=1)
    return h.astype(x.dtype), h[:, -1].astype(x.dtype)


def sb_block(q, k, v, bias, q_start):
    tq, tk = q.shape[1], k.shape[1]
    z = jnp.einsum('bqhd,bkhd->bhqk', q.astype(jnp.float32), k.astype(jnp.float32)) * (HEAD_DIM ** -0.5)
    z = z + bias.astype(jnp.float32)[None, :, None, None]
    qpos = q_start + jnp.arange(tq)
    kpos = jnp.arange(tk)
    mask = kpos[None, :] < qpos[:, None]
    log_1mb = jnp.where(mask, jax.nn.log_sigmoid(-z), 0.0)
    later = lax.cumsum(log_1mb, axis=3, reverse=True) - log_1mb
    attn = jnp.where(mask, jnp.exp(jax.nn.log_sigmoid(z) + later), 0.0)
    return jnp.einsum('bhqk,bkhd->bqhd', attn, v.astype(jnp.float32)).astype(q.dtype)


def stick_breaking(q, k, v, bias, q_start):
    B, tq, H, D = q.shape
    if tq > Q_BLOCK and tq % Q_BLOCK == 0:
        nb = tq // Q_BLOCK
        qb = q.reshape(B, nb, Q_BLOCK, H, D).transpose(1, 0, 2, 3, 4)
        starts = q_start + jnp.arange(nb, dtype=jnp.int32) * Q_BLOCK
        out = lax.map(lambda a: sb_block(a[0], k, v, bias, a[1]), (qb, starts))
        return out.transpose(1, 0, 2, 3, 4).reshape(B, tq, H, D)
    return sb_block(q, k, v, bias, q_start)


def trunk_layer(x, p, cconv_buf, lconv_buf, lru_h0, k_past, v_past):
    B, T, _ = x.shape
    h = rmsnorm(x, p['norm1_g'])
    cv, cg, lx, lg, q, k, v = jnp.split(h @ p['w_in'], SPLITS, axis=-1)
    glu = cv * jax.nn.sigmoid(cg)
    c, new_cconv = causal_dwconv(glu, cconv_buf, p['conv_dw_w'], p['conv_dw_b'])
    c = jax.nn.silu(layernorm(c, p['conv_ln_g'], p['conv_ln_b'])) @ p['conv_pw']
    xc, new_lconv = causal_dwconv(lx, lconv_buf, p['lru_conv_w'], p['lru_conv_b'])
    hr, new_h = rglru(xc, lru_h0, p['lru_wa'], p['lru_ba'], p['lru_wx'], p['lru_bx'], p['lru_lambda'])
    r_out = hr * jax.nn.gelu(lg)
    q = rmsnorm(q.reshape(B, T, SB_HEADS, HEAD_DIM), p['q_norm_g'])
    k = rmsnorm(k.reshape(B, T, SB_HEADS, HEAD_DIM), p['k_norm_g'])
    v = v.reshape(B, T, SB_HEADS, HEAD_DIM)
    if k_past is None:
        k_all, v_all, q_start = k, v, 0
    else:
        k_all = jnp.concatenate([k_past.astype(k.dtype), k], axis=1)
        v_all = jnp.concatenate([v_past.astype(v.dtype), v], axis=1)
        q_start = k_past.shape[1]
    a_out = stick_breaking(q, k_all, v_all, p['sb_bias'], q_start).reshape(B, T, SB_WIDTH)
    g = p['grp_norm_g']
    merged = jnp.concatenate([
        rmsnorm(c, g[:CONV_WIDTH]),
        rmsnorm(r_out, g[CONV_WIDTH:CONV_WIDTH + LRU_WIDTH]),
        rmsnorm(a_out, g[CONV_WIDTH + LRU_WIDTH:])], axis=-1)
    x = x + merged @ p['w_out']
    gt, up = jnp.split(rmsnorm(x, p['norm2_g']) @ p['w_gu'], 2, axis=-1)
    x = x + (jax.nn.silu(gt) * up) @ p['w_down']
    return x, new_cconv, new_lconv, new_h, k, v


def setup_inputs(seed: int = 0) -> dict:
    key = jax.random.key(seed)
    ks = jax.random.split(key, 40)
    n_pages = PAST_LEN // PAGE_SIZE
    used = DEC_BATCH * n_pages
    n_phys = used + max(1, used // 4)
    nrm = lambda k, shape, s: jax.random.normal(k, shape, jnp.float32) * s
    u = jax.random.uniform(ks[20], (DEPTH, LRU_WIDTH), jnp.float32, 0.9, 0.999)
    s = u ** (1.0 / LRU_C)
    lam = jnp.log(s) - jnp.log1p(-s)
    page_table = jax.random.permutation(ks[5], n_phys)[:used].reshape(DEC_BATCH, n_pages).astype(jnp.int32)
    return {
        'x_prompt': nrm(ks[0], (BATCH, SEQ, D_MODEL), 1.0),
        'x_sample': nrm(ks[1], (DEC_BATCH, DEC_SEQ, D_MODEL), 1.0),
        'cache_k': nrm(ks[2], (DEPTH, n_phys, PAGE_SIZE, SB_HEADS, HEAD_DIM), 1.0),
        'cache_v': nrm(ks[3], (DEPTH, n_phys, PAGE_SIZE, SB_HEADS, HEAD_DIM), 1.0),
        'page_table': page_table,
        'state_lru_h': nrm(ks[6], (DEPTH, DEC_BATCH, LRU_WIDTH), 0.5),
        'state_lru_conv': nrm(ks[7], (DEPTH, DEC_BATCH, LRU_CONV_K - 1, LRU_WIDTH), 1.0),
        'state_cconv': nrm(ks[8], (DEPTH, DEC_BATCH, CONV_K - 1, CONV_WIDTH), 0.5),
        'norm1_g': 1.0 + nrm(ks[9], (DEPTH, D_MODEL), 0.01),
        'w_in': nrm(ks[10], (DEPTH, D_MODEL, D_IN), D_MODEL ** -0.5),
        'conv_dw_w': nrm(ks[11], (DEPTH, CONV_K, CONV_WIDTH), CONV_K ** -0.5),
        'conv_dw_b': nrm(ks[12], (DEPTH, CONV_WIDTH), 0.01),
        'conv_ln_g': 1.0 + nrm(ks[13], (DEPTH, CONV_WIDTH), 0.01),
        'conv_ln_b': nrm(ks[14], (DEPTH, CONV_WIDTH), 0.01),
        'conv_pw': nrm(ks[15], (DEPTH, CONV_WIDTH, CONV_WIDTH), CONV_WIDTH ** -0.5),
        'lru_conv_w': nrm(ks[16], (DEPTH, LRU_CONV_K, LRU_WIDTH), LRU_CONV_K ** -0.5),
        'lru_conv_b': nrm(ks[17], (DEPTH, LRU_WIDTH), 0.01),
        'lru_wa': nrm(ks[18], (DEPTH, LRU_BLOCKS, LRU_BLOCK, LRU_BLOCK), LRU_BLOCK ** -0.5),
        'lru_ba': nrm(ks[19], (DEPTH, LRU_WIDTH), 0.01),
        'lru_wx': nrm(ks[21], (DEPTH, LRU_BLOCKS, LRU_BLOCK, LRU_BLOCK), LRU_BLOCK ** -0.5),
        'lru_bx': nrm(ks[22], (DEPTH, LRU_WIDTH), 0.01),
        'lru_lambda': lam,
        'q_norm_g': 1.0 + nrm(ks[23], (DEPTH, HEAD_DIM), 0.01),
        'k_norm_g': 1.0 + nrm(ks[24], (DEPTH, HEAD_DIM), 0.01),
        'sb_bias': SB_BIAS_INIT + nrm(ks[30], (DEPTH, SB_HEADS), 0.1),
        'grp_norm_g': 1.0 + nrm(ks[25], (DEPTH, D_MIX), 0.01),
        'w_out': nrm(ks[26], (DEPTH, D_MIX, D_MODEL), D_MIX ** -0.5),
        'norm2_g': 1.0 + nrm(ks[27], (DEPTH, D_MODEL), 0.01),
        'w_gu': nrm(ks[28], (DEPTH, D_MODEL, 2 * D_FF), D_MODEL ** -0.5),
        'w_down': nrm(ks[29], (DEPTH, D_FF, D_MODEL), D_FF ** -0.5),
    }


def reference(x_prompt, x_sample, cache_k, cache_v, page_table, state_lru_h, state_lru_conv, state_cconv,
              norm1_g, w_in, conv_dw_w, conv_dw_b, conv_ln_g, conv_ln_b, conv_pw,
              lru_conv_w, lru_conv_b, lru_wa, lru_ba, lru_wx, lru_bx, lru_lambda,
              q_norm_g, k_norm_g, sb_bias, grp_norm_g, w_out, norm2_g, w_gu, w_down):
    weights = {'norm1_g': norm1_g, 'w_in': w_in, 'conv_dw_w': conv_dw_w, 'conv_dw_b': conv_dw_b,
               'conv_ln_g': conv_ln_g, 'conv_ln_b': conv_ln_b, 'conv_pw': conv_pw,
               'lru_conv_w': lru_conv_w, 'lru_conv_b': lru_conv_b, 'lru_wa': lru_wa, 'lru_ba': lru_ba,
               'lru_wx': lru_wx, 'lru_bx': lru_bx, 'lru_lambda': lru_lambda,
               'q_norm_g': q_norm_g, 'k_norm_g': k_norm_g, 'sb_bias': sb_bias, 'grp_norm_g': grp_norm_g,
               'w_out': w_out, 'norm2_g': norm2_g, 'w_gu': w_gu, 'w_down': w_down}
    bp, bs = x_prompt.shape[0], x_sample.shape[0]
    xp, xs = x_prompt, x_sample
    kp_l, vp_l, ks_l, vs_l = [], [], [], []
    hp_l, hs_l, lcp_l, lcs_l, ccp_l, ccs_l = [], [], [], [], [], []
    for l in range(DEPTH):
        p = {name: arr[l] for name, arr in weights.items()}
        xp, ccp, lcp, hp, kp, vp = trunk_layer(
            xp, p,
            jnp.zeros((bp, CONV_K - 1, CONV_WIDTH), xp.dtype),
            jnp.zeros((bp, LRU_CONV_K - 1, LRU_WIDTH), xp.dtype),
            jnp.zeros((bp, LRU_WIDTH), xp.dtype), None, None)
        k_past = cache_k[l][page_table].reshape(bs, -1, SB_HEADS, HEAD_DIM)
        v_past = cache_v[l][page_table].reshape(bs, -1, SB_HEADS, HEAD_DIM)
        xs, ccs, lcs, hs, ksn, vsn = trunk_layer(
            xs, p, state_cconv[l], state_lru_conv[l], state_lru_h[l], k_past, v_past)
        kp_l.append(kp); vp_l.append(vp); ks_l.append(ksn); vs_l.append(vsn)
        hp_l.append(hp); hs_l.append(hs); lcp_l.append(lcp); lcs_l.append(lcs)
        ccp_l.append(ccp); ccs_l.append(ccs)
    return (xp, xs,
            jnp.stack(kp_l), jnp.stack(vp_l), jnp.stack(ks_l), jnp.stack(vs_l),
            jnp.stack(hp_l), jnp.stack(hs_l), jnp.stack(lcp_l), jnp.stack(lcs_l),
            jnp.stack(ccp_l), jnp.stack(ccs_l))
```

```python
import functools

import jax
import jax.numpy as jnp
from jax import lax
from jax.experimental import pallas as pl
from jax.experimental.pallas import tpu as pltpu

F32 = jnp.float32
BF16 = jnp.bfloat16

CONV_WIDTH = 512
CONV_K = 31
LRU_WIDTH = 512
LRU_BLOCKS = 4
LRU_BLOCK = LRU_WIDTH // LRU_BLOCKS
LRU_CONV_K = 4
LRU_C = 8.0
SB_HEADS = 8
HEAD_DIM = 128
SB_WIDTH = SB_HEADS * HEAD_DIM
EPS = 1e-6
LANES = 128
SUBLANES = 8
MIB = 1024 * 1024

COL_CV, COL_CG, COL_LX, COL_LG = 0, 1, 2, 3
COL_Q, COL_K, COL_V = 16, 24, 32


def _cparams(semantics, vmem_mib):
    return pltpu.CompilerParams(dimension_semantics=semantics, vmem_limit_bytes=vmem_mib * MIB)


def _rms(v, g):
    return v * lax.rsqrt(jnp.mean(v * v, axis=-1, keepdims=True) + EPS) * g


def _dot(a, b):
    return jnp.dot(a, b, preferred_element_type=F32)


def _softplus(z):
    return jnp.maximum(z, 0.0) + jnp.log(1.0 + jnp.exp(-jnp.abs(z)))


def _gelu_tanh(x):
    return 0.5 * x * (1.0 + jnp.tanh(0.7978845608028654 * (x + 0.044715 * (x * x * x))))


def _in_proj_body(x_ref, g_ref, w_ref, o_ref, xn_ref):
    @pl.when(pl.program_id(1) == 0)
    def _():
        xn_ref[...] = _rms(x_ref[...], g_ref[...]).astype(BF16)

    o_ref[...] = _dot(xn_ref[...], w_ref[...])


def _in_proj(x, g, w, tm, tn):
    m, d = x.shape
    n = w.shape[1]
    return pl.pallas_call(
        _in_proj_body,
        grid=(m // tm, n // tn),
        in_specs=[pl.BlockSpec((tm, d), lambda i, j: (i, 0)),
                  pl.BlockSpec((1, d), lambda i, j: (0, 0)),
                  pl.BlockSpec((d, tn), lambda i, j: (0, j))],
        out_specs=pl.BlockSpec((tm, tn), lambda i, j: (i, j)),
        out_shape=jax.ShapeDtypeStruct((m, n), F32),
        scratch_shapes=[pltpu.VMEM((tm, d), BF16)],
        compiler_params=_cparams(("parallel", "arbitrary"), 48),
        name="in_proj",
    )(x, g.reshape(1, d), w)


def _layernorm_silu(acc, g, b):
    mu = jnp.mean(acc, axis=-1, keepdims=True)
    xc = acc - mu
    var = jnp.mean(xc * xc, axis=-1, keepdims=True)
    y = xc * lax.rsqrt(var + EPS) * g + b
    return y * jax.nn.sigmoid(y)


def _conv_body(cv_ref, cg_ref, w_ref, b_ref, lng_ref, lnb_ref, pw_ref, c_ref, cc_ref, g_scr, *, tt, rc):
    halo = 32

    @pl.when(pl.program_id(1) == 0)
    def _():
        g_scr[0:halo, :] = jnp.zeros((halo, CONV_WIDTH), F32)

    g_scr[halo:halo + tt, :] = cv_ref[...] * jax.nn.sigmoid(cg_ref[...])
    first = halo - (CONV_K - 1)
    for r0 in range(0, tt, rc):
        acc = jnp.broadcast_to(b_ref[...], (rc, CONV_WIDTH))
        for j in range(CONV_K):
            acc = acc + w_ref[j:j + 1, :] * g_scr[r0 + first + j:r0 + first + j + rc, :]
        y = _layernorm_silu(acc, lng_ref[...], lnb_ref[...])
        c_ref[r0:r0 + rc, :] = _dot(y.astype(BF16), pw_ref[...])
    cc_ref[...] = g_scr[tt + first:tt + halo, :]
    g_scr[0:halo, :] = g_scr[tt:tt + halo, :]


def _conv_mixer(proj3, dw_w, dw_b, ln_g, ln_b, pw, tt):
    b, t, _ = proj3.shape
    cw = CONV_WIDTH
    row = lambda v: v.reshape(1, cw)
    const = lambda shape: pl.BlockSpec(shape, lambda bi, ti: (0,) * len(shape))
    return pl.pallas_call(
        functools.partial(_conv_body, tt=tt, rc=64),
        grid=(b, t // tt),
        in_specs=[pl.BlockSpec((None, tt, cw), lambda bi, ti: (bi, ti, COL_CV)),
                  pl.BlockSpec((None, tt, cw), lambda bi, ti: (bi, ti, COL_CG)),
                  const((CONV_K, cw)), const((1, cw)), const((1, cw)), const((1, cw)), const((cw, cw))],
        out_specs=[pl.BlockSpec((None, tt, cw), lambda bi, ti: (bi, ti, 0)),
                   pl.BlockSpec((None, CONV_K - 1, cw), lambda bi, ti: (bi, 0, 0))],
        out_shape=[jax.ShapeDtypeStruct((b, t, cw), F32),
                   jax.ShapeDtypeStruct((b, CONV_K - 1, cw), F32)],
        scratch_shapes=[pltpu.VMEM((32 + tt, cw), F32)],
        compiler_params=_cparams(("parallel", "arbitrary"), 32),
        name="conv_mixer",
    )(proj3, proj3, dw_w, row(dw_b), row(ln_g), row(ln_b), pw)


def _lru_gates(xc, wa_ref, ba_ref, wx_ref, bx_ref, sp):
    a_parts, u_parts = [], []
    for n in range(LRU_BLOCKS):
        sl = slice(n * LRU_BLOCK, (n + 1) * LRU_BLOCK)
        xb = xc[:, sl]
        xbb = xb.astype(BF16)
        r = jax.nn.sigmoid(_dot(xbb, wa_ref[n]) + ba_ref[:, sl])
        i = jax.nn.sigmoid(_dot(xbb, wx_ref[n]) + bx_ref[:, sl])
        a = jnp.exp(-LRU_C * r * sp[:, sl])
        a_parts.append(a)
        u_parts.append(jnp.sqrt(1.0 - a * a) * (i * xb))
    return a_parts, u_parts


def _lru_body(lx_ref, lg_ref, cw_ref, cb_ref, wa_ref, ba_ref, wx_ref, bx_ref, lam_ref,
              r_ref, lc_ref, h_ref, x_scr, a_scr, u_scr, h_scr, *, tt):
    halo = 8

    @pl.when(pl.program_id(1) == 0)
    def _():
        x_scr[0:halo, :] = jnp.zeros((halo, LRU_WIDTH), F32)
        h_scr[...] = jnp.zeros((1, LRU_WIDTH), F32)

    x_scr[halo:halo + tt, :] = lx_ref[...]
    first = halo - (LRU_CONV_K - 1)
    xc = jnp.broadcast_to(cb_ref[...], (tt, LRU_WIDTH))
    for j in range(LRU_CONV_K):
        xc = xc + cw_ref[j:j + 1, :] * x_scr[first + j:first + j + tt, :]
    sp = _softplus(-lam_ref[...])
    a_parts, u_parts = _lru_gates(xc, wa_ref, ba_ref, wx_ref, bx_ref, sp)
    for n in range(LRU_BLOCKS):
        sl = slice(n * LRU_BLOCK, (n + 1) * LRU_BLOCK)
        a_scr[:, sl] = a_parts[n]
        u_scr[:, sl] = u_parts[n]

    def step(i, h):
        h = a_scr[pl.ds(i, 1), :] * h + u_scr[pl.ds(i, 1), :]
        u_scr[pl.ds(i, 1), :] = h
        return h

    h = lax.fori_loop(0, tt, step, h_scr[...], unroll=8)
    h_scr[...] = h
    h_ref[...] = h
    r_ref[...] = u_scr[...] * _gelu_tanh(lg_ref[...])
    lc_ref[...] = x_scr[tt + first:tt + halo, :]
    x_scr[0:halo, :] = x_scr[tt:tt + halo, :]


def _lru_mixer(proj3, conv_w, conv_b, wa, ba, wx, bx, lam, tt):
    b, t, _ = proj3.shape
    lw = LRU_WIDTH
    row = lambda v: v.reshape(1, lw)
    const = lambda shape: pl.BlockSpec(shape, lambda bi, ti: (0,) * len(shape))
    gate = (LRU_BLOCKS, LRU_BLOCK, LRU_BLOCK)
    return pl.pallas_call(
        functools.partial(_lru_body, tt=tt),
        grid=(b, t // tt),
        in_specs=[pl.BlockSpec((None, tt, lw), lambda bi, ti: (bi, ti, COL_LX)),
                  pl.BlockSpec((None, tt, lw), lambda bi, ti: (bi, ti, COL_LG)),
                  const((LRU_CONV_K, lw)), const((1, lw)), const(gate), const((1, lw)),
                  const(gate), const((1, lw)), const((1, lw))],
        out_specs=[pl.BlockSpec((None, tt, lw), lambda bi, ti: (bi, ti, 0)),
                   pl.BlockSpec((None, LRU_CONV_K - 1, lw), lambda bi, ti: (bi, 0, 0)),
                   pl.BlockSpec((None, 1, lw), lambda bi, ti: (bi, 0, 0))],
        out_shape=[jax.ShapeDtypeStruct((b, t, lw), F32),
                   jax.ShapeDtypeStruct((b, LRU_CONV_K - 1, lw), F32),
                   jax.ShapeDtypeStruct((b, 1, lw), F32)],
        scratch_shapes=[pltpu.VMEM((8 + tt, lw), F32), pltpu.VMEM((tt, lw), F32),
                        pltpu.VMEM((tt, lw), F32), pltpu.VMEM((1, lw), F32)],
        compiler_params=_cparams(("parallel", "arbitrary"), 32),
        name="lru_mixer",
    )(proj3, proj3, conv_w, row(conv_b), wa, row(ba), wx, row(bx), row(lam))


def _split_bf16(x):
    hi = x.astype(BF16)
    return hi, (x - hi.astype(F32)).astype(BF16)


def _attn_body(q_ref, k_ref, v_ref, qg_ref, kg_ref, bias_ref, tri_ref, o_ref, ko_ref, vo_ref,
               kb_scr, vb_scr, *, tq):
    i = pl.program_id(2)

    @pl.when(i == 0)
    def _():
        kn = _rms(k_ref[...], kg_ref[...])
        ko_ref[...] = kn
        kb_scr[...] = kn.astype(BF16)
        v = v_ref[...]
        vo_ref[...] = v
        vb_scr[...] = v.astype(BF16)

    qb = _rms(q_ref[...], qg_ref[...]).astype(BF16)
    bias = bias_ref[...]
    tri = tri_ref[...]
    scale = HEAD_DIM ** -0.5

    def tile(start, run, acc, diagonal):
        s = lax.dot_general(qb, kb_scr[pl.ds(start, tq), :], (((1,), (1,)), ((), ())),
                            preferred_element_type=F32)
        z = s * scale + bias
        l = -_softplus(z)
        log_beta = z + l
        if diagonal:
            keep = (lax.broadcasted_iota(jnp.int32, (tq, tq), 1)
                    < lax.broadcasted_iota(jnp.int32, (tq, tq), 0))
            l = jnp.where(keep, l, 0.0)
        l_hi, l_lo = _split_bf16(l)
        later = _dot(l_hi, tri) + _dot(l_lo, tri)
        p = jnp.exp(log_beta + later + run)
        if diagonal:
            p = jnp.where(keep, p, 0.0)
        acc = acc + _dot(p.astype(BF16), vb_scr[pl.ds(start, tq), :])
        run = run + jnp.sum(l, axis=1, keepdims=True)
        return run, acc

    run, acc = tile(pl.multiple_of(i * tq, tq), jnp.zeros((tq, 1), F32),
                    jnp.zeros((tq, HEAD_DIM), F32), True)

    def body(jj, carry):
        j = i - 1 - jj
        return tile(pl.multiple_of(j * tq, tq), carry[0], carry[1], False)

    run, acc = lax.fori_loop(0, i, body, (run, acc))
    o_ref[...] = acc


def _sb_attn(proj3, q_g, k_g, bias, tq):
    b, t, _ = proj3.shape
    hd = HEAD_DIM
    tri = jnp.tril(jnp.ones((tq, tq), F32), -1).astype(BF16)
    bias_b = jnp.broadcast_to(bias.reshape(SB_HEADS, 1, 1), (SB_HEADS, 1, tq))
    seq = lambda col: pl.BlockSpec((None, t, hd), lambda bi, h, i: (bi, 0, col + h))
    const = lambda shape: pl.BlockSpec(shape, lambda bi, h, i: (0,) * len(shape))
    return pl.pallas_call(
        functools.partial(_attn_body, tq=tq),
        grid=(b, SB_HEADS, t // tq),
        in_specs=[pl.BlockSpec((None, tq, hd), lambda bi, h, i: (bi, i, COL_Q + h)),
                  seq(COL_K), seq(COL_V), const((1, hd)), const((1, hd)),
                  pl.BlockSpec((None, 1, tq), lambda bi, h, i: (h, 0, 0)), const((tq, tq))],
        out_specs=[pl.BlockSpec((None, tq, hd), lambda bi, h, i: (bi, i, h)), seq(0), seq(0)],
        out_shape=[jax.ShapeDtypeStruct((b, t, SB_WIDTH), F32)] * 3,
        scratch_shapes=[pltpu.VMEM((t, hd), BF16), pltpu.VMEM((t, hd), BF16)],
        compiler_params=_cparams(("parallel", "parallel", "arbitrary"), 48),
        name="sb_attn",
    )(proj3, proj3, proj3, q_g.reshape(1, hd), k_g.reshape(1, hd), bias_b, tri)


def _out_proj_body(c_ref, r_ref, a_ref, x_ref, g_ref, w_ref, o_ref):
    c0, c1, c2 = CONV_WIDTH, CONV_WIDTH + LRU_WIDTH, CONV_WIDTH + LRU_WIDTH + SB_WIDTH
    cn = _rms(c_ref[...], g_ref[:, 0:c0]).astype(BF16)
    rn = _rms(r_ref[...], g_ref[:, c0:c1]).astype(BF16)
    an = _rms(a_ref[...], g_ref[:, c1:c2]).astype(BF16)
    y = _dot(cn, w_ref[0:c0, :]) + _dot(rn, w_ref[c0:c1, :]) + _dot(an, w_ref[c1:c2, :])
    o_ref[...] = x_ref[...] + y


def _out_proj(c, r, a, x, g, w, tm):
    m, d = x.shape
    dm = w.shape[0]
    rows = lambda width: pl.BlockSpec((tm, width), lambda i: (i, 0))
    return pl.pallas_call(
        _out_proj_body,
        grid=(m // tm,),
        in_specs=[rows(CONV_WIDTH), rows(LRU_WIDTH), rows(SB_WIDTH), rows(d),
                  pl.BlockSpec((1, dm), lambda i: (0, 0)), pl.BlockSpec((dm, d), lambda i: (0, 0))],
        out_specs=rows(d),
        out_shape=jax.ShapeDtypeStruct((m, d), F32),
        compiler_params=_cparams(("parallel",), 56),
        name="out_proj",
    )(c, r, a, x, g.reshape(1, dm), w)


def _ffn_body(x_ref, g_ref, wg_ref, wu_ref, wd_ref, o_ref, xn_scr, acc_scr):
    f = pl.program_id(1)

    @pl.when(f == 0)
    def _():
        xn_scr[...] = _rms(x_ref[...], g_ref[...]).astype(BF16)
        acc_scr[...] = jnp.zeros_like(acc_scr)

    xn = xn_scr[...]
    gate = _dot(xn, wg_ref[...])
    up = _dot(xn, wu_ref[...])
    act = (gate * jax.nn.sigmoid(gate) * up).astype(BF16)
    acc_scr[...] += _dot(act, wd_ref[...])

    @pl.when(f == pl.num_programs(1) - 1)
    def _():
        o_ref[...] = x_ref[...] + acc_scr[...]


def _ffn(x, g, w_gu, w_down, tm, tf):
    m, d = x.shape
    dff = w_down.shape[0]
    nf = dff // tf
    return pl.pallas_call(
        _ffn_body,
        grid=(m // tm, nf),
        in_specs=[pl.BlockSpec((tm, d), lambda i, f: (i, 0)),
                  pl.BlockSpec((1, d), lambda i, f: (0, 0)),
                  pl.BlockSpec((d, tf), lambda i, f: (0, f)),
                  pl.BlockSpec((d, tf), lambda i, f: (0, nf + f)),
                  pl.BlockSpec((tf, d), lambda i, f: (f, 0))],
        out_specs=pl.BlockSpec((tm, d), lambda i, f: (i, 0)),
        out_shape=jax.ShapeDtypeStruct((m, d), F32),
        scratch_shapes=[pltpu.VMEM((tm, d), BF16), pltpu.VMEM((tm, d), F32)],
        compiler_params=_cparams(("parallel", "arbitrary"), 56),
        name="ffn",
    )(x, g.reshape(1, d), w_gu, w_gu, w_down)


def _dec_mix_body(pr_ref, cc_ref, lc_ref, h0_ref, dw_ref, dwb_ref, lng_ref, lnb_ref, pw_ref,
                  cw_ref, cb_ref, wa_ref, ba_ref, wx_ref, bx_ref, lam_ref, qg_ref, kg_ref,
                  c_ref, r_ref, q_ref, k_ref, v_ref, ccn_ref, lcn_ref, hn_ref, y_scr, x_scr, *, nb):
    cw, lw = CONV_WIDTH, LRU_WIDTH
    glu = pr_ref[:, 0:cw] * jax.nn.sigmoid(pr_ref[:, cw:2 * cw])
    lx = pr_ref[:, 2 * cw:2 * cw + lw]
    lg = pr_ref[:, 2 * cw + lw:2 * cw + 2 * lw]
    nc, nl = CONV_K - 1, LRU_CONV_K - 1
    for b in range(nb):
        y_scr[b:b + 1, :] = jnp.sum(cc_ref[b] * dw_ref[0:nc, :], axis=0, keepdims=True)
        ccn_ref[b, 0:nc - 1, :] = cc_ref[b, 1:nc, :]
        ccn_ref[b, nc - 1:nc, :] = glu[b:b + 1, :]
        x_scr[b:b + 1, :] = jnp.sum(lc_ref[b] * cw_ref[0:nl, :], axis=0, keepdims=True)
        lcn_ref[b, 0:nl - 1, :] = lc_ref[b, 1:nl, :]
        lcn_ref[b, nl - 1:nl, :] = lx[b:b + 1, :]
    acc = y_scr[...] + dw_ref[nc:nc + 1, :] * glu + dwb_ref[...]
    y = _layernorm_silu(acc, lng_ref[...], lnb_ref[...])
    c_ref[...] = _dot(y.astype(BF16), pw_ref[...])
    xc = x_scr[...] + cw_ref[nl:nl + 1, :] * lx + cb_ref[...]
    sp = _softplus(-lam_ref[...])
    a_parts, u_parts = _lru_gates(xc, wa_ref, ba_ref, wx_ref, bx_ref, sp)
    gl = _gelu_tanh(lg)
    for n in range(LRU_BLOCKS):
        sl = slice(n * LRU_BLOCK, (n + 1) * LRU_BLOCK)
        h = a_parts[n] * h0_ref[:, sl] + u_parts[n]
        hn_ref[:, sl] = h
        r_ref[:, sl] = h * gl[:, sl]
    base = 2 * cw + 2 * lw
    for hh in range(SB_HEADS):
        sl = slice(hh * HEAD_DIM, (hh + 1) * HEAD_DIM)
        q_ref[:, sl] = _rms(pr_ref[:, base + hh * HEAD_DIM:base + (hh + 1) * HEAD_DIM], qg_ref[...])
        k_ref[:, sl] = _rms(pr_ref[:, base + SB_WIDTH + hh * HEAD_DIM:base + SB_WIDTH + (hh + 1) * HEAD_DIM],
                            kg_ref[...])
    v_ref[...] = pr_ref[:, base + 2 * SB_WIDTH:base + 3 * SB_WIDTH]


def _dec_mix(pr, st_cc, st_lc, st_h, dw_w, dw_b, ln_g, ln_b, pw, conv_w, conv_b, wa, ba, wx, bx, lam, q_g, k_g):
    nb = pr.shape[0]
    cw, lw = CONV_WIDTH, LRU_WIDTH
    row = lambda v: v.reshape(1, -1)
    sd = jax.ShapeDtypeStruct
    return pl.pallas_call(
        functools.partial(_dec_mix_body, nb=nb),
        out_shape=[sd((nb, cw), F32), sd((nb, lw), F32), sd((nb, SB_WIDTH), F32), sd((nb, SB_WIDTH), F32),
                   sd((nb, SB_WIDTH), F32), sd((nb, CONV_K - 1, cw), F32), sd((nb, LRU_CONV_K - 1, lw), F32),
                   sd((nb, lw), F32)],
        scratch_shapes=[pltpu.VMEM((nb, cw), F32), pltpu.VMEM((nb, lw), F32)],
        compiler_params=pltpu.CompilerParams(vmem_limit_bytes=32 * MIB),
        name="dec_mix",
    )(pr, st_cc, st_lc, st_h, dw_w, row(dw_b), row(ln_g), row(ln_b), pw, conv_w, row(conv_b),
      wa, row(ba), wx, row(bx), row(lam), row(q_g), row(k_g))


def _dec_attn_body(pt_ref, q_ref, k_ref, v_ref, bias_ref, seg_ref, o_ref, zc_scr, run_scr, acc_scr, *, page):
    del pt_ref
    p = pl.program_id(1)
    nh, hd = SB_HEADS, HEAD_DIM
    rows = page * nh // LANES

    @pl.when(p == 0)
    def _():
        run_scr[...] = jnp.zeros_like(run_scr)
        acc_scr[...] = jnp.zeros_like(acc_scr)

    own = (lax.broadcasted_iota(jnp.int32, (nh, LANES), 1) & (nh - 1)) == lax.broadcasted_iota(
        jnp.int32, (nh, LANES), 0)
    qb = q_ref[...].astype(BF16)
    kb = k_ref[...].reshape(page * nh, hd).astype(BF16)
    z2 = lax.dot_general(qb, kb, (((1,), (1,)), ((), ())), preferred_element_type=F32)
    for r in range(rows):
        zc_scr[r:r + 1, :] = jnp.sum(jnp.where(own, z2[:, r * LANES:(r + 1) * LANES], 0.0), axis=0, keepdims=True)
    z = zc_scr[...] * (hd ** -0.5) + bias_ref[...]
    l = -_softplus(z)
    l_hi, l_lo = _split_bf16(l)
    seg = seg_ref[...]
    both = _dot(l_hi, seg) + _dot(l_lo, seg)
    later_row, row_tot = both[:, 0:LANES], both[:, LANES:2 * LANES]
    ridx = lax.broadcasted_iota(jnp.int32, (rows, LANES), 0)
    incl = row_tot
    sh = 1
    while sh < rows:
        incl = incl + jnp.where(ridx >= sh, pltpu.roll(incl, sh, 0), 0.0)
        sh *= 2
    total = jnp.sum(row_tot, axis=0, keepdims=True)
    later = later_row + (total - incl)
    w = jnp.exp(z + l + later + run_scr[...])
    a2 = jnp.concatenate(
        [jnp.where(own, jnp.broadcast_to(w[r:r + 1, :], (nh, LANES)), 0.0) for r in range(rows)], axis=1)
    vb = v_ref[...].reshape(page * nh, hd).astype(BF16)
    acc_scr[...] += _dot(a2.astype(BF16), vb)
    run_scr[...] += total

    @pl.when(p == pl.num_programs(1) - 1)
    def _():
        o_ref[...] = acc_scr[...]


def _dec_attn(q, cache_k, cache_v, page_table, bias, layer):
    nb = q.shape[0]
    n_pages = page_table.shape[1]
    page = cache_k.shape[2]
    nh, hd = SB_HEADS, HEAD_DIM
    lane = jnp.arange(LANES)
    same = (lane[:, None] % nh) == (lane[None, :] % nh)
    later = same & ((lane[:, None] // nh) > (lane[None, :] // nh))
    seg = jnp.concatenate([later, same], axis=1).astype(BF16)
    bias_l = jnp.tile(bias, LANES // nh).reshape(1, LANES)
    kv_spec = pl.BlockSpec((None, None, page, nh, hd),
                           lambda b, p, pt: (layer, pt[b, n_pages - 1 - p], 0, 0, 0))
    grid_spec = pltpu.PrefetchScalarGridSpec(
        num_scalar_prefetch=1,
        grid=(nb, n_pages),
        in_specs=[pl.BlockSpec((None, nh, hd), lambda b, p, pt: (b, 0, 0)), kv_spec, kv_spec,
                  pl.BlockSpec((1, LANES), lambda b, p, pt: (0, 0)),
                  pl.BlockSpec((LANES, 2 * LANES), lambda b, p, pt: (0, 0))],
        out_specs=pl.BlockSpec((None, nh, hd), lambda b, p, pt: (b, 0, 0)),
        scratch_shapes=[pltpu.VMEM((page * nh // LANES, LANES), F32), pltpu.VMEM((1, LANES), F32),
                        pltpu.VMEM((nh, hd), F32)],
    )
    return pl.pallas_call(
        functools.partial(_dec_attn_body, page=page),
        grid_spec=grid_spec,
        out_shape=jax.ShapeDtypeStruct((nb, nh, hd), F32),
        compiler_params=_cparams(("parallel", "arbitrary"), 32),
        name="dec_attn",
    )(page_table, q.reshape(nb, nh, hd), cache_k, cache_v, bias_l, seg)


def _prompt_layer(x, p, bsz, seq):
    d = x.shape[1]
    proj = _in_proj(x, p["norm1_g"], p["w_in"], tm=512, tn=512)
    proj3 = proj.reshape(bsz, seq, proj.shape[1])
    c, cc = _conv_mixer(proj3, p["conv_dw_w"], p["conv_dw_b"], p["conv_ln_g"], p["conv_ln_b"], p["conv_pw"], tt=512)
    r, lc, h = _lru_mixer(proj3, p["lru_conv_w"], p["lru_conv_b"], p["lru_wa"], p["lru_ba"], p["lru_wx"],
                          p["lru_bx"], p["lru_lambda"], tt=512)
    a, k, v = _sb_attn(proj3, p["q_norm_g"], p["k_norm_g"], p["sb_bias"], tq=256)
    m = bsz * seq
    x1 = _out_proj(c.reshape(m, -1), r.reshape(m, -1), a.reshape(m, -1), x, p["grp_norm_g"], p["w_out"], tm=512)
    x2 = _ffn(x1, p["norm2_g"], p["w_gu"], p["w_down"], tm=512, tf=512)
    return x2, cc, lc, h.reshape(bsz, -1), k, v


def _sample_layer(x, p, st_cc, st_lc, st_h, cache_k, cache_v, page_table, layer):
    nb = x.shape[0]
    proj = _in_proj(x, p["norm1_g"], p["w_in"], tm=nb, tn=1024)
    c, r, q, k, v, ccn, lcn, hn = _dec_mix(
        proj, st_cc, st_lc, st_h, p["conv_dw_w"], p["conv_dw_b"], p["conv_ln_g"], p["conv_ln_b"], p["conv_pw"],
        p["lru_conv_w"], p["lru_conv_b"], p["lru_wa"], p["lru_ba"], p["lru_wx"], p["lru_bx"], p["lru_lambda"],
        p["q_norm_g"], p["k_norm_g"])
    a = _dec_attn(q, cache_k, cache_v, page_table, p["sb_bias"], layer)
    x1 = _out_proj(c, r, a.reshape(nb, SB_WIDTH), x, p["grp_norm_g"], p["w_out"], tm=nb)
    x2 = _ffn(x1, p["norm2_g"], p["w_gu"], p["w_down"], tm=nb, tf=512)
    return x2, ccn, lcn, hn, k, v


_MATMUL_WEIGHTS = ("w_in", "conv_pw", "lru_wa", "lru_wx", "w_out", "w_gu", "w_down")


def kernel(x_prompt, x_sample, cache_k, cache_v, page_table, state_lru_h, state_lru_conv, state_cconv,
           norm1_g, w_in, conv_dw_w, conv_dw_b, conv_ln_g, conv_ln_b, conv_pw, lru_conv_w, lru_conv_b,
           lru_wa, lru_ba, lru_wx, lru_bx, lru_lambda, q_norm_g, k_norm_g, sb_bias, grp_norm_g, w_out,
           norm2_g, w_gu, w_down):
    weights = dict(norm1_g=norm1_g, w_in=w_in, conv_dw_w=conv_dw_w, conv_dw_b=conv_dw_b, conv_ln_g=conv_ln_g,
                   conv_ln_b=conv_ln_b, conv_pw=conv_pw, lru_conv_w=lru_conv_w, lru_conv_b=lru_conv_b,
                   lru_wa=lru_wa, lru_ba=lru_ba, lru_wx=lru_wx, lru_bx=lru_bx, lru_lambda=lru_lambda,
                   q_norm_g=q_norm_g, k_norm_g=k_norm_g, sb_bias=sb_bias, grp_norm_g=grp_norm_g, w_out=w_out,
                   norm2_g=norm2_g, w_gu=w_gu, w_down=w_down)
    for name in _MATMUL_WEIGHTS:
        weights[name] = weights[name].astype(BF16)
    depth = w_in.shape[0]
    bp, seq, d = x_prompt.shape
    bs = x_sample.shape[0]
    xp = x_prompt.reshape(bp * seq, d)
    xs = x_sample.reshape(bs, d)
    outs = [[] for _ in range(10)]
    for l in range(depth):
        p = {name: arr[l] for name, arr in weights.items()}
        xp, ccp, lcp, hp, kp, vp = _prompt_layer(xp, p, bp, seq)
        xs, ccs, lcs, hs, ksn, vsn = _sample_layer(xs, p, state_cconv[l], state_lru_conv[l], state_lru_h[l],
                                                   cache_k, cache_v, page_table, l)
        kv = lambda t, n: t.reshape(n, -1, SB_HEADS, HEAD_DIM)
        for lst, val in zip(outs, (kv(kp, bp), kv(vp, bp), kv(ksn, bs), kv(vsn, bs), hp, hs, lcp, lcs, ccp, ccs)):
            lst.append(val)
    return (xp.reshape(bp, seq, d), xs.reshape(bs, 1, d)) + tuple(jnp.stack(o) for o in outs)
```

```python
import functools

import jax
import jax.numpy as jnp
from jax import lax
from jax.experimental import pallas as pl
from jax.experimental.pallas import tpu as pltpu

F32 = jnp.float32
BF16 = jnp.bfloat16

CONV_WIDTH = 512
CONV_K = 31
LRU_WIDTH = 512
LRU_BLOCKS = 4
LRU_BLOCK = LRU_WIDTH // LRU_BLOCKS
LRU_CONV_K = 4
LRU_C = 8.0
SB_HEADS = 8
HEAD_DIM = 128
SB_WIDTH = SB_HEADS * HEAD_DIM
EPS = 1e-6
LANES = 128
SUBLANES = 8
MIB = 1024 * 1024

COL_CV, COL_CG, COL_LX, COL_LG = 0, 1, 2, 3
COL_Q, COL_K, COL_V = 2, 3, 4

ATTN_TQ = 1024
ATTN_TK = 256
DEC_PAGES = 8


def _cparams(semantics, vmem_mib):
    return pltpu.CompilerParams(dimension_semantics=semantics, vmem_limit_bytes=vmem_mib * MIB)


def _layer_spec(tail, layer):
    return pl.BlockSpec((None,) + tuple(tail), lambda *_: (layer,) + (0,) * len(tail))


def _rms(v, g):
    return v * lax.rsqrt(jnp.mean(v * v, axis=-1, keepdims=True) + EPS) * g


def _dot(a, b):
    return jnp.dot(a, b, preferred_element_type=F32)


def _dot_nt(a, b):
    return lax.dot_general(a, b, (((1,), (1,)), ((), ())), preferred_element_type=F32)


def _softplus(z):
    return jnp.maximum(z, 0.0) + jnp.log(1.0 + jnp.exp(-jnp.abs(z)))


def _gelu_tanh(x):
    return 0.5 * x * (1.0 + jnp.tanh(0.7978845608028654 * (x + 0.044715 * (x * x * x))))


def _split_bf16(x):
    hi = x.astype(BF16)
    return hi, (x - hi.astype(F32)).astype(BF16)


def _in_proj_body(x_ref, g_ref, w_ref, o_ref, xn_ref):
    @pl.when(pl.program_id(1) == 0)
    def _():
        xn_ref[...] = _rms(x_ref[...], g_ref[...]).astype(BF16)

    o_ref[...] = _dot(xn_ref[...], w_ref[...])


def _in_proj(x, g, w, layer, tm, tn):
    m, d = x.shape
    n = w.shape[2]
    return pl.pallas_call(
        _in_proj_body,
        grid=(m // tm, n // tn),
        in_specs=[pl.BlockSpec((tm, d), lambda i, j: (i, 0)),
                  _layer_spec((1, d), layer),
                  pl.BlockSpec((None, d, tn), lambda i, j: (layer, 0, j))],
        out_specs=pl.BlockSpec((tm, tn), lambda i, j: (i, j)),
        out_shape=jax.ShapeDtypeStruct((m, n), F32),
        scratch_shapes=[pltpu.VMEM((tm, d), BF16)],
        compiler_params=_cparams(("parallel", "arbitrary"), 56),
        name="in_proj",
    )(x, g, w)


def _layernorm_silu(acc, g, b):
    mu = jnp.mean(acc, axis=-1, keepdims=True)
    xc = acc - mu
    var = jnp.mean(xc * xc, axis=-1, keepdims=True)
    y = xc * lax.rsqrt(var + EPS) * g + b
    return y * jax.nn.sigmoid(y)


def _conv_body(cv_ref, cg_ref, w_ref, b_ref, lng_ref, lnb_ref, pw_ref, c_ref, cc_ref, g_scr, *, tt, rc):
    halo = 32

    @pl.when(pl.program_id(1) == 0)
    def _():
        g_scr[0:halo, :] = jnp.zeros((halo, CONV_WIDTH), F32)

    g_scr[halo:halo + tt, :] = cv_ref[...] * jax.nn.sigmoid(cg_ref[...])
    first = halo - (CONV_K - 1)
    for r0 in range(0, tt, rc):
        acc = jnp.broadcast_to(b_ref[...], (rc, CONV_WIDTH))
        for s in range(SUBLANES):
            ext = rc + (SUBLANES if s else 0)
            part = None
            for j in range(CONV_K):
                off = first + j
                if off % SUBLANES != s:
                    continue
                base = r0 + off - s
                term = w_ref[j:j + 1, :] * g_scr[base:base + ext, :]
                part = term if part is None else part + term
            acc = acc + (part[s:s + rc, :] if s else part)
        y = _layernorm_silu(acc, lng_ref[...], lnb_ref[...])
        c_ref[r0:r0 + rc, :] = _dot(y.astype(BF16), pw_ref[...])
    cc_ref[...] = g_scr[tt + first:tt + halo, :]
    g_scr[0:halo, :] = g_scr[tt:tt + halo, :]


def _conv_mixer(proj3, P, layer, tt):
    b, t, _ = proj3.shape
    cw = CONV_WIDTH
    row = _layer_spec((1, cw), layer)
    return pl.pallas_call(
        functools.partial(_conv_body, tt=tt, rc=64),
        grid=(b, t // tt),
        in_specs=[pl.BlockSpec((None, tt, cw), lambda bi, ti: (bi, ti, COL_CV)),
                  pl.BlockSpec((None, tt, cw), lambda bi, ti: (bi, ti, COL_CG)),
                  _layer_spec((CONV_K, cw), layer), row, row, row, _layer_spec((cw, cw), layer)],
        out_specs=[pl.BlockSpec((None, tt, cw), lambda bi, ti: (bi, ti, 0)),
                   pl.BlockSpec((None, CONV_K - 1, cw), lambda bi, ti: (bi, 0, 0))],
        out_shape=[jax.ShapeDtypeStruct((b, t, cw), F32),
                   jax.ShapeDtypeStruct((b, CONV_K - 1, cw), F32)],
        scratch_shapes=[pltpu.VMEM((32 + tt, cw), F32)],
        compiler_params=_cparams(("parallel", "arbitrary"), 32),
        name="conv_mixer",
    )(proj3, proj3, P["conv_dw_w"], P["conv_dw_b"], P["conv_ln_g"], P["conv_ln_b"], P["conv_pw"])


def _lru_gates(xc, wa_ref, ba_ref, wx_ref, bx_ref, sp):
    a_parts, u_parts = [], []
    for n in range(LRU_BLOCKS):
        sl = slice(n * LRU_BLOCK, (n + 1) * LRU_BLOCK)
        xb = xc[:, sl]
        xbb = xb.astype(BF16)
        r = jax.nn.sigmoid(_dot(xbb, wa_ref[n]) + ba_ref[:, sl])
        i = jax.nn.sigmoid(_dot(xbb, wx_ref[n]) + bx_ref[:, sl])
        a = jnp.exp(-LRU_C * r * sp[:, sl])
        a_parts.append(a)
        u_parts.append(jnp.sqrt(1.0 - a * a) * (i * xb))
    return a_parts, u_parts


def _lru_body(lx_ref, lg_ref, cw_ref, cb_ref, wa_ref, ba_ref, wx_ref, bx_ref, lam_ref,
              r_ref, lc_ref, h_ref, x_scr, a_scr, u_scr, h_scr, *, tt):
    halo = 8

    @pl.when(pl.program_id(1) == 0)
    def _():
        x_scr[0:halo, :] = jnp.zeros((halo, LRU_WIDTH), F32)
        h_scr[...] = jnp.zeros((1, LRU_WIDTH), F32)

    x_scr[halo:halo + tt, :] = lx_ref[...]
    first = halo - (LRU_CONV_K - 1)
    xc = jnp.broadcast_to(cb_ref[...], (tt, LRU_WIDTH))
    for j in range(LRU_CONV_K):
        xc = xc + cw_ref[j:j + 1, :] * x_scr[first + j:first + j + tt, :]
    sp = _softplus(-lam_ref[...])
    a_parts, u_parts = _lru_gates(xc, wa_ref, ba_ref, wx_ref, bx_ref, sp)
    for n in range(LRU_BLOCKS):
        sl = slice(n * LRU_BLOCK, (n + 1) * LRU_BLOCK)
        a_scr[:, sl] = a_parts[n]
        u_scr[:, sl] = u_parts[n]

    def step(i, h):
        h = a_scr[pl.ds(i, 1), :] * h + u_scr[pl.ds(i, 1), :]
        u_scr[pl.ds(i, 1), :] = h
        return h

    h = lax.fori_loop(0, tt, step, h_scr[...], unroll=8)
    h_scr[...] = h
    h_ref[...] = h
    r_ref[...] = u_scr[...] * _gelu_tanh(lg_ref[...])
    lc_ref[...] = x_scr[tt + first:tt + halo, :]
    x_scr[0:halo, :] = x_scr[tt:tt + halo, :]


def _lru_mixer(proj3, P, layer, tt):
    b, t, _ = proj3.shape
    lw = LRU_WIDTH
    row = _layer_spec((1, lw), layer)
    gate = _layer_spec((LRU_BLOCKS, LRU_BLOCK, LRU_BLOCK), layer)
    return pl.pallas_call(
        functools.partial(_lru_body, tt=tt),
        grid=(b, t // tt),
        in_specs=[pl.BlockSpec((None, tt, lw), lambda bi, ti: (bi, ti, COL_LX)),
                  pl.BlockSpec((None, tt, lw), lambda bi, ti: (bi, ti, COL_LG)),
                  _layer_spec((LRU_CONV_K, lw), layer), row, gate, row, gate, row, row],
        out_specs=[pl.BlockSpec((None, tt, lw), lambda bi, ti: (bi, ti, 0)),
                   pl.BlockSpec((None, LRU_CONV_K - 1, lw), lambda bi, ti: (bi, 0, 0)),
                   pl.BlockSpec((None, 1, lw), lambda bi, ti: (bi, 0, 0))],
        out_shape=[jax.ShapeDtypeStruct((b, t, lw), F32),
                   jax.ShapeDtypeStruct((b, LRU_CONV_K - 1, lw), F32),
                   jax.ShapeDtypeStruct((b, 1, lw), F32)],
        scratch_shapes=[pltpu.VMEM((8 + tt, lw), F32), pltpu.VMEM((tt, lw), F32),
                        pltpu.VMEM((tt, lw), F32), pltpu.VMEM((1, lw), F32)],
        compiler_params=_cparams(("parallel", "arbitrary"), 32),
        name="lru_mixer",
    )(proj3, proj3, P["lru_conv_w"], P["lru_conv_b"], P["lru_wa"], P["lru_ba"], P["lru_wx"], P["lru_bx"],
      P["lru_lambda"])


def _qkv_body(q_ref, k_ref, v_ref, qg_ref, kg_ref, qb_ref, kb_ref, vb_ref, ko_ref, vo_ref):
    for h in range(SB_HEADS):
        sl = slice(h * HEAD_DIM, (h + 1) * HEAD_DIM)
        qb_ref[:, sl] = _rms(q_ref[:, sl], qg_ref[...]).astype(BF16)
        kn = _rms(k_ref[:, sl], kg_ref[...])
        kb_ref[:, sl] = kn.astype(BF16)
        ko_ref[:, h, :] = kn
        v = v_ref[:, sl]
        vb_ref[:, sl] = v.astype(BF16)
        vo_ref[:, h, :] = v


def _qkv_prep(proj3, P, layer, tt):
    b, t, _ = proj3.shape
    w = SB_WIDTH
    col = lambda c: pl.BlockSpec((None, tt, w), lambda bi, ti: (bi, ti, c))
    o2 = pl.BlockSpec((None, tt, w), lambda bi, ti: (bi, ti, 0))
    o4 = pl.BlockSpec((None, tt, SB_HEADS, HEAD_DIM), lambda bi, ti: (bi, ti, 0, 0))
    sd = jax.ShapeDtypeStruct
    return pl.pallas_call(
        _qkv_body,
        grid=(b, t // tt),
        in_specs=[col(COL_Q), col(COL_K), col(COL_V), _layer_spec((1, HEAD_DIM), layer),
                  _layer_spec((1, HEAD_DIM), layer)],
        out_specs=[o2, o2, o2, o4, o4],
        out_shape=[sd((b, t, w), BF16)] * 3 + [sd((b, t, SB_HEADS, HEAD_DIM), F32)] * 2,
        compiler_params=_cparams(("parallel", "parallel"), 48),
        name="qkv_prep",
    )(proj3, proj3, proj3, P["q_norm_g"], P["k_norm_g"])


def _attn_body(q_ref, k_ref, v_ref, bias_ref, tri_ref, o_ref, run_scr, acc_scr, *, tq, tk):
    i = pl.program_id(2)
    nsub = tq // tk
    scale = HEAD_DIM ** -0.5
    run_scr[...] = jnp.zeros_like(run_scr)
    acc_scr[...] = jnp.zeros_like(acc_scr)

    def tile(r0, m, kstart, diagonal):
        s = _dot_nt(q_ref[r0:r0 + m, :], k_ref[pl.ds(kstart, tk), :])
        z = s * scale + bias_ref[...]
        sp = _softplus(z)
        spm = sp
        if diagonal:
            keep = (lax.broadcasted_iota(jnp.int32, (m, tk), 1) < lax.broadcasted_iota(jnp.int32, (m, tk), 0))
            spm = jnp.where(keep, sp, 0.0)
        hi, lo = _split_bf16(spm)
        later = _dot(jnp.concatenate([hi, lo], axis=1), tri_ref[...])
        run = run_scr[r0:r0 + m, :]
        p = jnp.exp(z - sp + later - run)
        if diagonal:
            p = jnp.where(keep, p, 0.0)
        acc_scr[r0:r0 + m, :] += _dot(p.astype(BF16), v_ref[pl.ds(kstart, tk), :])
        run_scr[r0:r0 + m, :] = run + jnp.sum(spm, axis=1, keepdims=True)

    for c in reversed(range(nsub)):
        tile(c * tk, tq - c * tk, pl.multiple_of(i * tq + c * tk, tk), True)

    def body(jj, carry):
        tile(0, tq, pl.multiple_of((i * nsub - 1 - jj) * tk, tk), False)
        return carry

    lax.fori_loop(0, i * nsub, body, 0)
    o_ref[...] = acc_scr[...]


def _sb_attn(qb, kb, vb, P, layer, tq, tk):
    b, t, w = qb.shape
    hd = HEAD_DIM
    seq = pl.BlockSpec((None, t, hd), lambda bi, h, i: (bi, 0, h))
    return pl.pallas_call(
        functools.partial(_attn_body, tq=tq, tk=tk),
        grid=(b, SB_HEADS, t // tq),
        in_specs=[pl.BlockSpec((None, tq, hd), lambda bi, h, i: (bi, i, h)), seq, seq,
                  pl.BlockSpec((None, None, 1, tk), lambda bi, h, i: (layer, h, 0, 0)),
                  pl.BlockSpec((2 * tk, tk), lambda bi, h, i: (0, 0))],
        out_specs=pl.BlockSpec((None, tq, hd), lambda bi, h, i: (bi, i, h)),
        out_shape=jax.ShapeDtypeStruct((b, t, w), F32),
        scratch_shapes=[pltpu.VMEM((tq, 1), F32), pltpu.VMEM((tq, hd), F32)],
        compiler_params=_cparams(("parallel", "parallel", "arbitrary"), 56),
        name="sb_attn",
    )(qb, kb, vb, P["bias_prompt"], P["tri"])


def _out_proj_body(c_ref, r_ref, a_ref, x_ref, g_ref, w_ref, o_ref):
    c0, c1, c2 = CONV_WIDTH, CONV_WIDTH + LRU_WIDTH, CONV_WIDTH + LRU_WIDTH + SB_WIDTH
    cn = _rms(c_ref[...], g_ref[:, 0:c0]).astype(BF16)
    rn = _rms(r_ref[...], g_ref[:, c0:c1]).astype(BF16)
    an = _rms(a_ref[...], g_ref[:, c1:c2]).astype(BF16)
    y = _dot(cn, w_ref[0:c0, :]) + _dot(rn, w_ref[c0:c1, :]) + _dot(an, w_ref[c1:c2, :])
    o_ref[...] = x_ref[...] + y


def _out_proj(c, r, a, x, P, layer, tm):
    m, d = x.shape
    dm = P["w_out"].shape[1]
    rows = lambda width: pl.BlockSpec((tm, width), lambda i: (i, 0))
    return pl.pallas_call(
        _out_proj_body,
        grid=(m // tm,),
        in_specs=[rows(CONV_WIDTH), rows(LRU_WIDTH), rows(SB_WIDTH), rows(d),
                  _layer_spec((1, dm), layer), _layer_spec((dm, d), layer)],
        out_specs=rows(d),
        out_shape=jax.ShapeDtypeStruct((m, d), F32),
        compiler_params=_cparams(("parallel",), 56),
        name="out_proj",
    )(c, r, a, x, P["grp_norm_g"], P["w_out"])


def _ffn_body(x_ref, g_ref, wg_ref, wu_ref, wd_ref, o_ref, xn_scr, acc_scr):
    f = pl.program_id(1)

    @pl.when(f == 0)
    def _():
        xn_scr[...] = _rms(x_ref[...], g_ref[...]).astype(BF16)
        acc_scr[...] = jnp.zeros_like(acc_scr)

    xn = xn_scr[...]
    gate = _dot(xn, wg_ref[...])
    up = _dot(xn, wu_ref[...])
    act = (gate * jax.nn.sigmoid(gate) * up).astype(BF16)
    acc_scr[...] += _dot(act, wd_ref[...])

    @pl.when(f == pl.num_programs(1) - 1)
    def _():
        o_ref[...] = x_ref[...] + acc_scr[...]


def _ffn(x, P, layer, tm, tf):
    m, d = x.shape
    dff = P["w_down"].shape[1]
    nf = dff // tf
    return pl.pallas_call(
        _ffn_body,
        grid=(m // tm, nf),
        in_specs=[pl.BlockSpec((tm, d), lambda i, f: (i, 0)),
                  _layer_spec((1, d), layer),
                  pl.BlockSpec((None, d, tf), lambda i, f: (layer, 0, f)),
                  pl.BlockSpec((None, d, tf), lambda i, f: (layer, 0, nf + f)),
                  pl.BlockSpec((None, tf, d), lambda i, f: (layer, f, 0))],
        out_specs=pl.BlockSpec((tm, d), lambda i, f: (i, 0)),
        out_shape=jax.ShapeDtypeStruct((m, d), F32),
        scratch_shapes=[pltpu.VMEM((tm, d), BF16), pltpu.VMEM((tm, d), F32)],
        compiler_params=_cparams(("parallel", "arbitrary"), 56),
        name="ffn",
    )(x, P["norm2_g"], P["w_gu"], P["w_gu"], P["w_down"])


def _dec_mix_body(pr_ref, cc_ref, lc_ref, h0_ref, dw_ref, dwb_ref, lng_ref, lnb_ref, pw_ref,
                  cw_ref, cb_ref, wa_ref, ba_ref, wx_ref, bx_ref, lam_ref, qg_ref, kg_ref,
                  c_ref, r_ref, q_ref, k_ref, v_ref, ccn_ref, lcn_ref, hn_ref, y_scr, x_scr, *, nb):
    cw, lw = CONV_WIDTH, LRU_WIDTH
    glu = pr_ref[:, 0:cw] * jax.nn.sigmoid(pr_ref[:, cw:2 * cw])
    lx = pr_ref[:, 2 * cw:2 * cw + lw]
    lg = pr_ref[:, 2 * cw + lw:2 * cw + 2 * lw]
    nc, nl = CONV_K - 1, LRU_CONV_K - 1
    for b in range(nb):
        y_scr[b:b + 1, :] = jnp.sum(cc_ref[b] * dw_ref[0:nc, :], axis=0, keepdims=True)
        ccn_ref[b, 0:nc - 1, :] = cc_ref[b, 1:nc, :]
        ccn_ref[b, nc - 1:nc, :] = glu[b:b + 1, :]
        x_scr[b:b + 1, :] = jnp.sum(lc_ref[b] * cw_ref[0:nl, :], axis=0, keepdims=True)
        lcn_ref[b, 0:nl - 1, :] = lc_ref[b, 1:nl, :]
        lcn_ref[b, nl - 1:nl, :] = lx[b:b + 1, :]
    acc = y_scr[...] + dw_ref[nc:nc + 1, :] * glu + dwb_ref[...]
    y = _layernorm_silu(acc, lng_ref[...], lnb_ref[...])
    c_ref[...] = _dot(y.astype(BF16), pw_ref[...])
    xc = x_scr[...] + cw_ref[nl:nl + 1, :] * lx + cb_ref[...]
    sp = _softplus(-lam_ref[...])
    a_parts, u_parts = _lru_gates(xc, wa_ref, ba_ref, wx_ref, bx_ref, sp)
    gl = _gelu_tanh(lg)
    for n in range(LRU_BLOCKS):
        sl = slice(n * LRU_BLOCK, (n + 1) * LRU_BLOCK)
        h = a_parts[n] * h0_ref[:, sl] + u_parts[n]
        hn_ref[:, sl] = h
        r_ref[:, sl] = h * gl[:, sl]
    base = 2 * cw + 2 * lw
    for hh in range(SB_HEADS):
        sl = slice(hh * HEAD_DIM, (hh + 1) * HEAD_DIM)
        q_ref[:, sl] = _rms(pr_ref[:, base + hh * HEAD_DIM:base + (hh + 1) * HEAD_DIM], qg_ref[...])
        k_ref[:, sl] = _rms(pr_ref[:, base + SB_WIDTH + hh * HEAD_DIM:base + SB_WIDTH + (hh + 1) * HEAD_DIM],
                            kg_ref[...])
    v_ref[...] = pr_ref[:, base + 2 * SB_WIDTH:base + 3 * SB_WIDTH]


def _dec_mix(pr, st_cc, st_lc, st_h, P, layer):
    nb, n_in = pr.shape
    cw, lw = CONV_WIDTH, LRU_WIDTH
    nc, nl = CONV_K - 1, LRU_CONV_K - 1
    sd = jax.ShapeDtypeStruct
    whole = lambda shape: pl.BlockSpec(shape, lambda i: (0,) * len(shape))
    rowc, rowl, rowh = _layer_spec((1, cw), layer), _layer_spec((1, lw), layer), _layer_spec((1, HEAD_DIM), layer)
    gate = _layer_spec((LRU_BLOCKS, LRU_BLOCK, LRU_BLOCK), layer)
    return pl.pallas_call(
        functools.partial(_dec_mix_body, nb=nb),
        grid=(1,),
        in_specs=[whole((nb, n_in)), _layer_spec((nb, nc, cw), layer), _layer_spec((nb, nl, lw), layer),
                  _layer_spec((nb, lw), layer),
                  _layer_spec((CONV_K, cw), layer), rowc, rowc, rowc, _layer_spec((cw, cw), layer),
                  _layer_spec((LRU_CONV_K, lw), layer), rowl, gate, rowl, gate, rowl, rowl, rowh, rowh],
        out_specs=[whole((nb, cw)), whole((nb, lw)), whole((nb, SB_WIDTH)), whole((nb, SB_WIDTH)),
                   whole((nb, SB_WIDTH)), whole((nb, nc, cw)), whole((nb, nl, lw)), whole((nb, lw))],
        out_shape=[sd((nb, cw), F32), sd((nb, lw), F32), sd((nb, SB_WIDTH), F32), sd((nb, SB_WIDTH), F32),
                   sd((nb, SB_WIDTH), F32), sd((nb, nc, cw), F32), sd((nb, nl, lw), F32), sd((nb, lw), F32)],
        scratch_shapes=[pltpu.VMEM((nb, cw), F32), pltpu.VMEM((nb, lw), F32)],
        compiler_params=_cparams(("arbitrary",), 32),
        name="dec_mix",
    )(pr, st_cc, st_lc, st_h, P["conv_dw_w"], P["conv_dw_b"], P["conv_ln_g"], P["conv_ln_b"], P["conv_pw"],
      P["lru_conv_w"], P["lru_conv_b"], P["lru_wa"], P["lru_ba"], P["lru_wx"], P["lru_bx"], P["lru_lambda"],
      P["q_norm_g"], P["k_norm_g"])


def _dec_attn_body(pt_ref, q_ref, *refs, page, npp):
    del pt_ref
    k_refs, v_refs = refs[0:npp], refs[npp:2 * npp]
    bias_ref, seg_ref, blk_ref, o_ref, zc_scr, run_scr, acc_scr = refs[2 * npp:]
    p = pl.program_id(1)
    nh, hd = SB_HEADS, HEAD_DIM
    nr = page * nh // LANES * npp

    @pl.when(p == 0)
    def _():
        run_scr[...] = jnp.zeros_like(run_scr)
        acc_scr[...] = jnp.zeros_like(acc_scr)

    own = (lax.broadcasted_iota(jnp.int32, (nh, LANES), 1) & (nh - 1)) == lax.broadcasted_iota(
        jnp.int32, (nh, LANES), 0)
    qb = q_ref[...].astype(BF16)
    kb = jnp.concatenate([k_refs[u][...].reshape(page * nh, hd).astype(BF16) for u in range(npp)], axis=0)
    z2 = _dot_nt(qb, kb)
    for r in range(nr):
        zc_scr[r:r + 1, :] = jnp.sum(jnp.where(own, z2[:, r * LANES:(r + 1) * LANES], 0.0), axis=0, keepdims=True)
    z = zc_scr[...] * (hd ** -0.5) + bias_ref[...]
    sp = _softplus(z)
    hi, lo = _split_bf16(sp)
    both = _dot(hi, seg_ref[...]) + _dot(lo, seg_ref[...])
    later_row, row_tot = both[:, 0:LANES], both[:, LANES:2 * LANES]
    thi, tlo = _split_bf16(row_tot)
    later_blk = _dot(blk_ref[...], thi) + _dot(blk_ref[...], tlo)
    w = jnp.exp(z - sp - later_row - later_blk - run_scr[...])
    a2 = jnp.concatenate(
        [jnp.where(own, jnp.broadcast_to(w[r:r + 1, :], (nh, LANES)), 0.0) for r in range(nr)], axis=1)
    vb = jnp.concatenate([v_refs[u][...].reshape(page * nh, hd).astype(BF16) for u in range(npp)], axis=0)
    acc_scr[...] += _dot(a2.astype(BF16), vb)
    run_scr[...] += jnp.sum(row_tot, axis=0, keepdims=True)

    @pl.when(p == pl.num_programs(1) - 1)
    def _():
        o_ref[...] = acc_scr[...]


def _dec_attn(q, cache_k, cache_v, page_table, P, layer, npp):
    nb = q.shape[0]
    n_pages = page_table.shape[1]
    page = cache_k.shape[2]
    nh, hd = SB_HEADS, HEAD_DIM
    nr = page * nh // LANES * npp

    def kv_spec(u):
        return pl.BlockSpec((None, None, page, nh, hd),
                            lambda b, p, pt: (layer, pt[b, n_pages - 1 - (p * npp + u)], 0, 0, 0))

    const = lambda shape: pl.BlockSpec(shape, lambda b, p, pt: (0,) * len(shape))
    grid_spec = pltpu.PrefetchScalarGridSpec(
        num_scalar_prefetch=1,
        grid=(nb, n_pages // npp),
        in_specs=[pl.BlockSpec((None, nh, hd), lambda b, p, pt: (b, 0, 0))]
        + [kv_spec(u) for u in range(npp)] + [kv_spec(u) for u in range(npp)]
        + [pl.BlockSpec((None, 1, LANES), lambda b, p, pt: (layer, 0, 0)), const((LANES, 2 * LANES)),
           const((nr, nr))],
        out_specs=pl.BlockSpec((None, nh, hd), lambda b, p, pt: (b, 0, 0)),
        scratch_shapes=[pltpu.VMEM((nr, LANES), F32), pltpu.VMEM((1, LANES), F32), pltpu.VMEM((nh, hd), F32)],
    )
    return pl.pallas_call(
        functools.partial(_dec_attn_body, page=page, npp=npp),
        grid_spec=grid_spec,
        out_shape=jax.ShapeDtypeStruct((nb, nh, hd), F32),
        compiler_params=_cparams(("parallel", "arbitrary"), 48),
        name="dec_attn",
    )(page_table, q.reshape(nb, nh, hd), *([cache_k] * npp), *([cache_v] * npp), P["bias_dec"], P["seg"],
      P["blk"])


_MATMUL_WEIGHTS = ("w_in", "conv_pw", "lru_wa", "lru_wx", "w_out", "w_gu", "w_down")
_ROW_PARAMS = ("norm1_g", "conv_dw_b", "conv_ln_g", "conv_ln_b", "lru_conv_b", "lru_ba", "lru_bx", "lru_lambda",
               "q_norm_g", "k_norm_g", "grp_norm_g", "norm2_g")


def _prepare(weights, page, tk, npp):
    P = dict(weights)
    for name in _MATMUL_WEIGHTS:
        P[name] = weights[name].astype(BF16)
    for name in _ROW_PARAMS:
        v = weights[name]
        P[name] = v.reshape(v.shape[0], 1, v.shape[1])
    bias = weights["sb_bias"]
    depth, nh = bias.shape
    P["bias_prompt"] = jnp.broadcast_to(bias.reshape(depth, nh, 1, 1), (depth, nh, 1, tk))
    P["bias_dec"] = jnp.tile(bias, (1, LANES // nh)).reshape(depth, 1, LANES)
    tri = jnp.tril(jnp.ones((tk, tk), F32), -1)
    P["tri"] = (-jnp.concatenate([tri, tri], axis=0)).astype(BF16)
    lane = jnp.arange(LANES)
    same = (lane[:, None] % nh) == (lane[None, :] % nh)
    later = same & ((lane[:, None] // nh) > (lane[None, :] // nh))
    P["seg"] = jnp.concatenate([later, same], axis=1).astype(BF16)
    rows = page * nh // LANES
    ri = jnp.arange(rows * npp)
    grp = ri // rows
    P["blk"] = (((grp[None, :] == grp[:, None]) & (ri[None, :] > ri[:, None]))
                | (grp[None, :] < grp[:, None])).astype(BF16)
    return P


def _prompt_layer(x, P, layer, bsz, seq):
    proj = _in_proj(x, P["norm1_g"], P["w_in"], layer, tm=min(1024, bsz * seq), tn=1024)
    proj3 = proj.reshape(bsz, seq, proj.shape[1])
    c, cc = _conv_mixer(proj3, P, layer, tt=512)
    r, lc, h = _lru_mixer(proj3, P, layer, tt=512)
    qb, kb, vb, k, v = _qkv_prep(proj3, P, layer, tt=512)
    a = _sb_attn(qb, kb, vb, P, layer, tq=min(ATTN_TQ, seq), tk=ATTN_TK)
    m = bsz * seq
    x1 = _out_proj(c.reshape(m, -1), r.reshape(m, -1), a.reshape(m, -1), x, P, layer, tm=512)
    x2 = _ffn(x1, P, layer, tm=512, tf=512)
    return x2, cc, lc, h.reshape(bsz, -1), k, v


def _sample_layer(x, P, st_cc, st_lc, st_h, cache_k, cache_v, page_table, layer):
    nb = x.shape[0]
    proj = _in_proj(x, P["norm1_g"], P["w_in"], layer, tm=nb, tn=1024)
    c, r, q, k, v, ccn, lcn, hn = _dec_mix(proj, st_cc, st_lc, st_h, P, layer)
    a = _dec_attn(q, cache_k, cache_v, page_table, P, layer, min(DEC_PAGES, page_table.shape[1]))
    x1 = _out_proj(c, r, a.reshape(nb, SB_WIDTH), x, P, layer, tm=nb)
    x2 = _ffn(x1, P, layer, tm=nb, tf=512)
    return x2, ccn, lcn, hn, k, v


def kernel(x_prompt, x_sample, cache_k, cache_v, page_table, state_lru_h, state_lru_conv, state_cconv,
           norm1_g, w_in, conv_dw_w, conv_dw_b, conv_ln_g, conv_ln_b, conv_pw, lru_conv_w, lru_conv_b,
           lru_wa, lru_ba, lru_wx, lru_bx, lru_lambda, q_norm_g, k_norm_g, sb_bias, grp_norm_g, w_out,
           norm2_g, w_gu, w_down):
    weights = dict(norm1_g=norm1_g, w_in=w_in, conv_dw_w=conv_dw_w, conv_dw_b=conv_dw_b, conv_ln_g=conv_ln_g,
                   conv_ln_b=conv_ln_b, conv_pw=conv_pw, lru_conv_w=lru_conv_w, lru_conv_b=lru_conv_b,
                   lru_wa=lru_wa, lru_ba=lru_ba, lru_wx=lru_wx, lru_bx=lru_bx, lru_lambda=lru_lambda,
                   q_norm_g=q_norm_g, k_norm_g=k_norm_g, sb_bias=sb_bias, grp_norm_g=grp_norm_g, w_out=w_out,
                   norm2_g=norm2_g, w_gu=w_gu, w_down=w_down)
    depth = w_in.shape[0]
    bp, seq, d = x_prompt.shape
    bs = x_sample.shape[0]
    P = _prepare(weights, cache_k.shape[2], ATTN_TK, min(DEC_PAGES, page_table.shape[1]))
    xp = x_prompt.reshape(bp * seq, d)
    xs = x_sample.reshape(bs, d)
    outs = [[] for _ in range(10)]
    for l in range(depth):
        xp, ccp, lcp, hp, kp, vp = _prompt_layer(xp, P, l, bp, seq)
        xs, ccs, lcs, hs, ksn, vsn = _sample_layer(xs, P, state_cconv, state_lru_conv, state_lru_h,
                                                   cache_k, cache_v, page_table, l)
        kv = lambda t: t.reshape(bs, -1, SB_HEADS, HEAD_DIM)
        for lst, val in zip(outs, (kp, vp, kv(ksn), kv(vsn), hp, hs, lcp, lcs, ccp, ccs)):
            lst.append(val)
    return (xp.reshape(bp, seq, d), xs.reshape(bs, 1, d)) + tuple(jnp.stack(o) for o in outs)
```

```python
import functools

import jax
import jax.numpy as jnp
from jax import lax
from jax.experimental import pallas as pl
from jax.experimental.pallas import tpu as pltpu

F32 = jnp.float32
BF16 = jnp.bfloat16

CONV_WIDTH = 512
CONV_K = 31
LRU_WIDTH = 512
LRU_BLOCKS = 4
LRU_BLOCK = LRU_WIDTH // LRU_BLOCKS
LRU_CONV_K = 4
LRU_C = 8.0
SB_HEADS = 8
HEAD_DIM = 128
SB_WIDTH = SB_HEADS * HEAD_DIM
EPS = 1e-6
LANES = 128
SUBLANES = 8
MIB = 1024 * 1024

COL_CV, COL_CG, COL_LX, COL_LG = 0, 1, 2, 3
COL_Q, COL_K, COL_V = 2, 3, 4

ATTN_TQ = 2048
ATTN_TK = 256
DEC_PAGES = 8


def _cparams(semantics, vmem_mib):
    return pltpu.CompilerParams(dimension_semantics=semantics, vmem_limit_bytes=vmem_mib * MIB)


def _layer_spec(tail, layer):
    return pl.BlockSpec((None,) + tuple(tail), lambda *_: (layer,) + (0,) * len(tail))


def _rms(v, g):
    return v * lax.rsqrt(jnp.mean(v * v, axis=-1, keepdims=True) + EPS) * g


def _dot(a, b):
    return jnp.dot(a, b, preferred_element_type=F32)


def _dot_nt(a, b):
    return lax.dot_general(a, b, (((1,), (1,)), ((), ())), preferred_element_type=F32)


def _softplus(z):
    return jnp.maximum(z, 0.0) + jnp.log(1.0 + jnp.exp(-jnp.abs(z)))


SOFTPLUS_LINEAR_ABOVE = 80.0


def _softplus_fast(z):
    return jnp.maximum(z, jnp.log(1.0 + jnp.exp(jnp.minimum(z, SOFTPLUS_LINEAR_ABOVE))))


def _gelu_tanh(x):
    return 0.5 * x * (1.0 + jnp.tanh(0.7978845608028654 * (x + 0.044715 * (x * x * x))))


def _split_bf16(x):
    hi = x.astype(BF16)
    return hi, (x - hi.astype(F32)).astype(BF16)


def _in_proj_body(x_ref, g_ref, w_ref, o_ref, xn_ref):
    @pl.when(pl.program_id(1) == 0)
    def _():
        xn_ref[...] = _rms(x_ref[...], g_ref[...]).astype(BF16)

    o_ref[...] = _dot(xn_ref[...], w_ref[...])


def _in_proj(x, g, w, layer, tm, tn):
    m, d = x.shape
    n = w.shape[2]
    return pl.pallas_call(
        _in_proj_body,
        grid=(m // tm, n // tn),
        in_specs=[pl.BlockSpec((tm, d), lambda i, j: (i, 0)),
                  _layer_spec((1, d), layer),
                  pl.BlockSpec((None, d, tn), lambda i, j: (layer, 0, j))],
        out_specs=pl.BlockSpec((tm, tn), lambda i, j: (i, j)),
        out_shape=jax.ShapeDtypeStruct((m, n), F32),
        scratch_shapes=[pltpu.VMEM((tm, d), BF16)],
        compiler_params=_cparams(("parallel", "arbitrary"), 56),
        name="in_proj",
    )(x, g, w)


def _layernorm_silu(acc, g, b):
    mu = jnp.mean(acc, axis=-1, keepdims=True)
    xc = acc - mu
    var = jnp.mean(xc * xc, axis=-1, keepdims=True)
    y = xc * lax.rsqrt(var + EPS) * g + b
    return y * jax.nn.sigmoid(y)


def _conv_body(cv_ref, cg_ref, w_ref, b_ref, lng_ref, lnb_ref, pw_ref, c_ref, cc_ref, g_scr, *, tt, rc):
    halo = 32

    @pl.when(pl.program_id(1) == 0)
    def _():
        g_scr[0:halo, :] = jnp.zeros((halo, CONV_WIDTH), F32)

    g_scr[halo:halo + tt, :] = cv_ref[...] * jax.nn.sigmoid(cg_ref[...])
    first = halo - (CONV_K - 1)
    for r0 in range(0, tt, rc):
        acc = jnp.broadcast_to(b_ref[...], (rc, CONV_WIDTH))
        for s in range(SUBLANES):
            ext = rc + (SUBLANES if s else 0)
            part = None
            for j in range(CONV_K):
                off = first + j
                if off % SUBLANES != s:
                    continue
                base = r0 + off - s
                term = w_ref[j:j + 1, :] * g_scr[base:base + ext, :]
                part = term if part is None else part + term
            acc = acc + (part[s:s + rc, :] if s else part)
        y = _layernorm_silu(acc, lng_ref[...], lnb_ref[...])
        c_ref[r0:r0 + rc, :] = _dot(y.astype(BF16), pw_ref[...])
    cc_ref[...] = g_scr[tt + first:tt + halo, :]
    g_scr[0:halo, :] = g_scr[tt:tt + halo, :]


def _conv_mixer(proj3, P, layer, tt):
    b, t, _ = proj3.shape
    cw = CONV_WIDTH
    row = _layer_spec((1, cw), layer)
    return pl.pallas_call(
        functools.partial(_conv_body, tt=tt, rc=64),
        grid=(b, t // tt),
        in_specs=[pl.BlockSpec((None, tt, cw), lambda bi, ti: (bi, ti, COL_CV)),
                  pl.BlockSpec((None, tt, cw), lambda bi, ti: (bi, ti, COL_CG)),
                  _layer_spec((CONV_K, cw), layer), row, row, row, _layer_spec((cw, cw), layer)],
        out_specs=[pl.BlockSpec((None, tt, cw), lambda bi, ti: (bi, ti, 0)),
                   pl.BlockSpec((None, CONV_K - 1, cw), lambda bi, ti: (bi, 0, 0))],
        out_shape=[jax.ShapeDtypeStruct((b, t, cw), F32),
                   jax.ShapeDtypeStruct((b, CONV_K - 1, cw), F32)],
        scratch_shapes=[pltpu.VMEM((32 + tt, cw), F32)],
        compiler_params=_cparams(("parallel", "arbitrary"), 32),
        name="conv_mixer",
    )(proj3, proj3, P["conv_dw_w"], P["conv_dw_b"], P["conv_ln_g"], P["conv_ln_b"], P["conv_pw"])


def _lru_gates(xc, wa_ref, ba_ref, wx_ref, bx_ref, sp):
    a_parts, u_parts = [], []
    for n in range(LRU_BLOCKS):
        sl = slice(n * LRU_BLOCK, (n + 1) * LRU_BLOCK)
        xb = xc[:, sl]
        xbb = xb.astype(BF16)
        r = jax.nn.sigmoid(_dot(xbb, wa_ref[n]) + ba_ref[:, sl])
        i = jax.nn.sigmoid(_dot(xbb, wx_ref[n]) + bx_ref[:, sl])
        a = jnp.exp(-LRU_C * r * sp[:, sl])
        a_parts.append(a)
        u_parts.append(jnp.sqrt(1.0 - a * a) * (i * xb))
    return a_parts, u_parts


def _lru_body(lx_ref, lg_ref, cw_ref, cb_ref, wa_ref, ba_ref, wx_ref, bx_ref, lam_ref,
              r_ref, lc_ref, h_ref, x_scr, a_scr, u_scr, h_scr, *, tt):
    halo = 8

    @pl.when(pl.program_id(1) == 0)
    def _():
        x_scr[0:halo, :] = jnp.zeros((halo, LRU_WIDTH), F32)
        h_scr[...] = jnp.zeros((1, LRU_WIDTH), F32)

    x_scr[halo:halo + tt, :] = lx_ref[...]
    first = halo - (LRU_CONV_K - 1)
    xc = jnp.broadcast_to(cb_ref[...], (tt, LRU_WIDTH))
    for j in range(LRU_CONV_K):
        xc = xc + cw_ref[j:j + 1, :] * x_scr[first + j:first + j + tt, :]
    sp = _softplus(-lam_ref[...])
    a_parts, u_parts = _lru_gates(xc, wa_ref, ba_ref, wx_ref, bx_ref, sp)
    for n in range(LRU_BLOCKS):
        sl = slice(n * LRU_BLOCK, (n + 1) * LRU_BLOCK)
        a_scr[:, sl] = a_parts[n]
        u_scr[:, sl] = u_parts[n]

    def step(i, h):
        h = a_scr[pl.ds(i, 1), :] * h + u_scr[pl.ds(i, 1), :]
        u_scr[pl.ds(i, 1), :] = h
        return h

    h = lax.fori_loop(0, tt, step, h_scr[...], unroll=8)
    h_scr[...] = h
    h_ref[...] = h
    r_ref[...] = u_scr[...] * _gelu_tanh(lg_ref[...])
    lc_ref[...] = x_scr[tt + first:tt + halo, :]
    x_scr[0:halo, :] = x_scr[tt:tt + halo, :]


def _lru_mixer(proj3, P, layer, tt):
    b, t, _ = proj3.shape
    lw = LRU_WIDTH
    row = _layer_spec((1, lw), layer)
    gate = _layer_spec((LRU_BLOCKS, LRU_BLOCK, LRU_BLOCK), layer)
    return pl.pallas_call(
        functools.partial(_lru_body, tt=tt),
        grid=(b, t // tt),
        in_specs=[pl.BlockSpec((None, tt, lw), lambda bi, ti: (bi, ti, COL_LX)),
                  pl.BlockSpec((None, tt, lw), lambda bi, ti: (bi, ti, COL_LG)),
                  _layer_spec((LRU_CONV_K, lw), layer), row, gate, row, gate, row, row],
        out_specs=[pl.BlockSpec((None, tt, lw), lambda bi, ti: (bi, ti, 0)),
                   pl.BlockSpec((None, LRU_CONV_K - 1, lw), lambda bi, ti: (bi, 0, 0)),
                   pl.BlockSpec((None, 1, lw), lambda bi, ti: (bi, 0, 0))],
        out_shape=[jax.ShapeDtypeStruct((b, t, lw), F32),
                   jax.ShapeDtypeStruct((b, LRU_CONV_K - 1, lw), F32),
                   jax.ShapeDtypeStruct((b, 1, lw), F32)],
        scratch_shapes=[pltpu.VMEM((8 + tt, lw), F32), pltpu.VMEM((tt, lw), F32),
                        pltpu.VMEM((tt, lw), F32), pltpu.VMEM((1, lw), F32)],
        compiler_params=_cparams(("parallel", "arbitrary"), 32),
        name="lru_mixer",
    )(proj3, proj3, P["lru_conv_w"], P["lru_conv_b"], P["lru_wa"], P["lru_ba"], P["lru_wx"], P["lru_bx"],
      P["lru_lambda"])


def _qkv_body(q_ref, k_ref, v_ref, qg_ref, kg_ref, qb_ref, kb_ref, vb_ref, ko_ref, vo_ref):
    for h in range(SB_HEADS):
        sl = slice(h * HEAD_DIM, (h + 1) * HEAD_DIM)
        qb_ref[:, sl] = _rms(q_ref[:, sl], qg_ref[...]).astype(BF16)
        kn = _rms(k_ref[:, sl], kg_ref[...])
        kb_ref[:, sl] = kn.astype(BF16)
        ko_ref[:, h, :] = kn
        v = v_ref[:, sl]
        vb_ref[:, sl] = v.astype(BF16)
        vo_ref[:, h, :] = v


def _qkv_prep(proj3, P, layer, tt):
    b, t, _ = proj3.shape
    w = SB_WIDTH
    col = lambda c: pl.BlockSpec((None, tt, w), lambda bi, ti: (bi, ti, c))
    o2 = pl.BlockSpec((None, tt, w), lambda bi, ti: (bi, ti, 0))
    o4 = pl.BlockSpec((None, tt, SB_HEADS, HEAD_DIM), lambda bi, ti: (bi, ti, 0, 0))
    sd = jax.ShapeDtypeStruct
    return pl.pallas_call(
        _qkv_body,
        grid=(b, t // tt),
        in_specs=[col(COL_Q), col(COL_K), col(COL_V), _layer_spec((1, HEAD_DIM), layer),
                  _layer_spec((1, HEAD_DIM), layer)],
        out_specs=[o2, o2, o2, o4, o4],
        out_shape=[sd((b, t, w), BF16)] * 3 + [sd((b, t, SB_HEADS, HEAD_DIM), F32)] * 2,
        compiler_params=_cparams(("parallel", "parallel"), 48),
        name="qkv_prep",
    )(proj3, proj3, proj3, P["q_norm_g"], P["k_norm_g"])


def _attn_body(q_ref, k_ref, v_ref, bias_ref, tri_ref, o_ref, run_scr, acc_scr, *, tq, tk):
    i = pl.program_id(2)
    nsub = tq // tk
    scale = HEAD_DIM ** -0.5
    run_scr[...] = jnp.zeros_like(run_scr)
    acc_scr[...] = jnp.zeros_like(acc_scr)

    def tile(r0, m, kstart, diagonal):
        s = _dot_nt(q_ref[r0:r0 + m, :], k_ref[pl.ds(kstart, tk), :])
        z = s * scale + bias_ref[...]
        sp = _softplus_fast(z)
        spm = sp
        if diagonal:
            keep = (lax.broadcasted_iota(jnp.int32, (m, tk), 1) < lax.broadcasted_iota(jnp.int32, (m, tk), 0))
            spm = jnp.where(keep, sp, 0.0)
        hi, lo = _split_bf16(spm)
        later = _dot(jnp.concatenate([hi, lo], axis=1), tri_ref[...])
        run = run_scr[r0:r0 + m, :]
        p = jnp.exp(z - sp + later - run)
        if diagonal:
            p = jnp.where(keep, p, 0.0)
        acc_scr[r0:r0 + m, :] += _dot(p.astype(BF16), v_ref[pl.ds(kstart, tk), :])
        run_scr[r0:r0 + m, :] = run + (spm[:, 0:1] - later[:, 0:1])

    for c in reversed(range(nsub)):
        tile(c * tk, tq - c * tk, pl.multiple_of(i * tq + c * tk, tk), True)

    def body(jj, carry):
        tile(0, tq, pl.multiple_of((i * nsub - 1 - jj) * tk, tk), False)
        return carry

    lax.fori_loop(0, i * nsub, body, 0)
    o_ref[...] = acc_scr[...]


def _sb_attn(qb, kb, vb, P, layer, tq, tk):
    b, t, w = qb.shape
    hd = HEAD_DIM
    seq = pl.BlockSpec((None, t, hd), lambda bi, h, i: (bi, 0, h))
    return pl.pallas_call(
        functools.partial(_attn_body, tq=tq, tk=tk),
        grid=(b, SB_HEADS, t // tq),
        in_specs=[pl.BlockSpec((None, tq, hd), lambda bi, h, i: (bi, i, h)), seq, seq,
                  pl.BlockSpec((None, None, 1, tk), lambda bi, h, i: (layer, h, 0, 0)),
                  pl.BlockSpec((2 * tk, tk), lambda bi, h, i: (0, 0))],
        out_specs=pl.BlockSpec((None, tq, hd), lambda bi, h, i: (bi, i, h)),
        out_shape=jax.ShapeDtypeStruct((b, t, w), F32),
        scratch_shapes=[pltpu.VMEM((tq, 1), F32), pltpu.VMEM((tq, hd), F32)],
        compiler_params=_cparams(("parallel", "parallel", "arbitrary"), 56),
        name="sb_attn",
    )(qb, kb, vb, P["bias_prompt"], P["tri"])


def _out_proj_body(c_ref, r_ref, a_ref, x_ref, g_ref, w_ref, o_ref):
    c0, c1, c2 = CONV_WIDTH, CONV_WIDTH + LRU_WIDTH, CONV_WIDTH + LRU_WIDTH + SB_WIDTH
    cn = _rms(c_ref[...], g_ref[:, 0:c0]).astype(BF16)
    rn = _rms(r_ref[...], g_ref[:, c0:c1]).astype(BF16)
    an = _rms(a_ref[...], g_ref[:, c1:c2]).astype(BF16)
    y = _dot(cn, w_ref[0:c0, :]) + _dot(rn, w_ref[c0:c1, :]) + _dot(an, w_ref[c1:c2, :])
    o_ref[...] = x_ref[...] + y


def _out_proj(c, r, a, x, P, layer, tm):
    m, d = x.shape
    dm = P["w_out"].shape[1]
    rows = lambda width: pl.BlockSpec((tm, width), lambda i: (i, 0))
    return pl.pallas_call(
        _out_proj_body,
        grid=(m // tm,),
        in_specs=[rows(CONV_WIDTH), rows(LRU_WIDTH), rows(SB_WIDTH), rows(d),
                  _layer_spec((1, dm), layer), _layer_spec((dm, d), layer)],
        out_specs=rows(d),
        out_shape=jax.ShapeDtypeStruct((m, d), F32),
        compiler_params=_cparams(("parallel",), 56),
        name="out_proj",
    )(c, r, a, x, P["grp_norm_g"], P["w_out"])


def _ffn_body(x_ref, g_ref, wg_ref, wu_ref, wd_ref, o_ref, xn_scr, acc_scr):
    f = pl.program_id(1)

    @pl.when(f == 0)
    def _():
        xn_scr[...] = _rms(x_ref[...], g_ref[...]).astype(BF16)
        acc_scr[...] = jnp.zeros_like(acc_scr)

    xn = xn_scr[...]
    gate = _dot(xn, wg_ref[...])
    up = _dot(xn, wu_ref[...])
    act = (gate * jax.nn.sigmoid(gate) * up).astype(BF16)
    acc_scr[...] += _dot(act, wd_ref[...])

    @pl.when(f == pl.num_programs(1) - 1)
    def _():
        o_ref[...] = x_ref[...] + acc_scr[...]


def _ffn(x, P, layer, tm, tf):
    m, d = x.shape
    dff = P["w_down"].shape[1]
    nf = dff // tf
    return pl.pallas_call(
        _ffn_body,
        grid=(m // tm, nf),
        in_specs=[pl.BlockSpec((tm, d), lambda i, f: (i, 0)),
                  _layer_spec((1, d), layer),
                  pl.BlockSpec((None, d, tf), lambda i, f: (layer, 0, f)),
                  pl.BlockSpec((None, d, tf), lambda i, f: (layer, 0, nf + f)),
                  pl.BlockSpec((None, tf, d), lambda i, f: (layer, f, 0))],
        out_specs=pl.BlockSpec((tm, d), lambda i, f: (i, 0)),
        out_shape=jax.ShapeDtypeStruct((m, d), F32),
        scratch_shapes=[pltpu.VMEM((tm, d), BF16), pltpu.VMEM((tm, d), F32)],
        compiler_params=_cparams(("parallel", "arbitrary"), 56),
        name="ffn",
    )(x, P["norm2_g"], P["w_gu"], P["w_gu"], P["w_down"])


def _dec_mix_body(pr_ref, cc_ref, lc_ref, h0_ref, dw_ref, dwb_ref, lng_ref, lnb_ref, pw_ref,
                  cw_ref, cb_ref, wa_ref, ba_ref, wx_ref, bx_ref, lam_ref, qg_ref, kg_ref,
                  c_ref, r_ref, q_ref, k_ref, v_ref, ccn_ref, lcn_ref, hn_ref, y_scr, x_scr, *, nb):
    cw, lw = CONV_WIDTH, LRU_WIDTH
    glu = pr_ref[:, 0:cw] * jax.nn.sigmoid(pr_ref[:, cw:2 * cw])
    lx = pr_ref[:, 2 * cw:2 * cw + lw]
    lg = pr_ref[:, 2 * cw + lw:2 * cw + 2 * lw]
    nc, nl = CONV_K - 1, LRU_CONV_K - 1
    for b in range(nb):
        y_scr[b:b + 1, :] = jnp.sum(cc_ref[b] * dw_ref[0:nc, :], axis=0, keepdims=True)
        ccn_ref[b, 0:nc - 1, :] = cc_ref[b, 1:nc, :]
        ccn_ref[b, nc - 1:nc, :] = glu[b:b + 1, :]
        x_scr[b:b + 1, :] = jnp.sum(lc_ref[b] * cw_ref[0:nl, :], axis=0, keepdims=True)
        lcn_ref[b, 0:nl - 1, :] = lc_ref[b, 1:nl, :]
        lcn_ref[b, nl - 1:nl, :] = lx[b:b + 1, :]
    acc = y_scr[...] + dw_ref[nc:nc + 1, :] * glu + dwb_ref[...]
    y = _layernorm_silu(acc, lng_ref[...], lnb_ref[...])
    c_ref[...] = _dot(y.astype(BF16), pw_ref[...])
    xc = x_scr[...] + cw_ref[nl:nl + 1, :] * lx + cb_ref[...]
    sp = _softplus(-lam_ref[...])
    a_parts, u_parts = _lru_gates(xc, wa_ref, ba_ref, wx_ref, bx_ref, sp)
    gl = _gelu_tanh(lg)
    for n in range(LRU_BLOCKS):
        sl = slice(n * LRU_BLOCK, (n + 1) * LRU_BLOCK)
        h = a_parts[n] * h0_ref[:, sl] + u_parts[n]
        hn_ref[:, sl] = h
        r_ref[:, sl] = h * gl[:, sl]
    base = 2 * cw + 2 * lw
    for hh in range(SB_HEADS):
        sl = slice(hh * HEAD_DIM, (hh + 1) * HEAD_DIM)
        q_ref[:, sl] = _rms(pr_ref[:, base + hh * HEAD_DIM:base + (hh + 1) * HEAD_DIM], qg_ref[...])
        k_ref[:, sl] = _rms(pr_ref[:, base + SB_WIDTH + hh * HEAD_DIM:base + SB_WIDTH + (hh + 1) * HEAD_DIM],
                            kg_ref[...])
    v_ref[...] = pr_ref[:, base + 2 * SB_WIDTH:base + 3 * SB_WIDTH]


def _dec_mix(pr, st_cc, st_lc, st_h, P, layer):
    nb, n_in = pr.shape
    cw, lw = CONV_WIDTH, LRU_WIDTH
    nc, nl = CONV_K - 1, LRU_CONV_K - 1
    sd = jax.ShapeDtypeStruct
    whole = lambda shape: pl.BlockSpec(shape, lambda i: (0,) * len(shape))
    rowc, rowl, rowh = _layer_spec((1, cw), layer), _layer_spec((1, lw), layer), _layer_spec((1, HEAD_DIM), layer)
    gate = _layer_spec((LRU_BLOCKS, LRU_BLOCK, LRU_BLOCK), layer)
    return pl.pallas_call(
        functools.partial(_dec_mix_body, nb=nb),
        grid=(1,),
        in_specs=[whole((nb, n_in)), _layer_spec((nb, nc, cw), layer), _layer_spec((nb, nl, lw), layer),
                  _layer_spec((nb, lw), layer),
                  _layer_spec((CONV_K, cw), layer), rowc, rowc, rowc, _layer_spec((cw, cw), layer),
                  _layer_spec((LRU_CONV_K, lw), layer), rowl, gate, rowl, gate, rowl, rowl, rowh, rowh],
        out_specs=[whole((nb, cw)), whole((nb, lw)), whole((nb, SB_WIDTH)), whole((nb, SB_WIDTH)),
                   whole((nb, SB_WIDTH)), whole((nb, nc, cw)), whole((nb, nl, lw)), whole((nb, lw))],
        out_shape=[sd((nb, cw), F32), sd((nb, lw), F32), sd((nb, SB_WIDTH), F32), sd((nb, SB_WIDTH), F32),
                   sd((nb, SB_WIDTH), F32), sd((nb, nc, cw), F32), sd((nb, nl, lw), F32), sd((nb, lw), F32)],
        scratch_shapes=[pltpu.VMEM((nb, cw), F32), pltpu.VMEM((nb, lw), F32)],
        compiler_params=_cparams(("arbitrary",), 32),
        name="dec_mix",
    )(pr, st_cc, st_lc, st_h, P["conv_dw_w"], P["conv_dw_b"], P["conv_ln_g"], P["conv_ln_b"], P["conv_pw"],
      P["lru_conv_w"], P["lru_conv_b"], P["lru_wa"], P["lru_ba"], P["lru_wx"], P["lru_bx"], P["lru_lambda"],
      P["q_norm_g"], P["k_norm_g"])


def _dec_attn_body(pt_ref, q_ref, *refs, page, npp):
    del pt_ref
    k_refs, v_refs = refs[0:npp], refs[npp:2 * npp]
    bias_ref, seg_ref, blk_ref, o_ref, zc_scr, run_scr, acc_scr = refs[2 * npp:]
    p = pl.program_id(1)
    nh, hd = SB_HEADS, HEAD_DIM
    nr = page * nh // LANES * npp

    @pl.when(p == 0)
    def _():
        run_scr[...] = jnp.zeros_like(run_scr)
        acc_scr[...] = jnp.zeros_like(acc_scr)

    own = (lax.broadcasted_iota(jnp.int32, (nh, LANES), 1) & (nh - 1)) == lax.broadcasted_iota(
        jnp.int32, (nh, LANES), 0)
    qb = q_ref[...].astype(BF16)
    kb = jnp.concatenate([k_refs[u][...].reshape(page * nh, hd).astype(BF16) for u in range(npp)], axis=0)
    z2 = _dot_nt(qb, kb)
    for r in range(nr):
        zc_scr[r:r + 1, :] = jnp.sum(jnp.where(own, z2[:, r * LANES:(r + 1) * LANES], 0.0), axis=0, keepdims=True)
    z = zc_scr[...] * (hd ** -0.5) + bias_ref[...]
    sp = _softplus(z)
    hi, lo = _split_bf16(sp)
    both = _dot(hi, seg_ref[...]) + _dot(lo, seg_ref[...])
    later_row, row_tot = both[:, 0:LANES], both[:, LANES:2 * LANES]
    thi, tlo = _split_bf16(row_tot)
    later_blk = _dot(blk_ref[...], thi) + _dot(blk_ref[...], tlo)
    w = jnp.exp(z - sp - later_row - later_blk - run_scr[...])
    a2 = jnp.concatenate(
        [jnp.where(own, jnp.broadcast_to(w[r:r + 1, :], (nh, LANES)), 0.0) for r in range(nr)], axis=1)
    vb = jnp.concatenate([v_refs[u][...].reshape(page * nh, hd).astype(BF16) for u in range(npp)], axis=0)
    acc_scr[...] += _dot(a2.astype(BF16), vb)
    run_scr[...] += jnp.sum(row_tot, axis=0, keepdims=True)

    @pl.when(p == pl.num_programs(1) - 1)
    def _():
        o_ref[...] = acc_scr[...]


def _dec_attn(q, cache_k, cache_v, page_table, P, layer, npp):
    nb = q.shape[0]
    n_pages = page_table.shape[1]
    page = cache_k.shape[2]
    nh, hd = SB_HEADS, HEAD_DIM
    nr = page * nh // LANES * npp

    def kv_spec(u):
        return pl.BlockSpec((None, None, page, nh, hd),
                            lambda b, p, pt: (layer, pt[b, n_pages - 1 - (p * npp + u)], 0, 0, 0))

    const = lambda shape: pl.BlockSpec(shape, lambda b, p, pt: (0,) * len(shape))
    grid_spec = pltpu.PrefetchScalarGridSpec(
        num_scalar_prefetch=1,
        grid=(nb, n_pages // npp),
        in_specs=[pl.BlockSpec((None, nh, hd), lambda b, p, pt: (b, 0, 0))]
        + [kv_spec(u) for u in range(npp)] + [kv_spec(u) for u in range(npp)]
        + [pl.BlockSpec((None, 1, LANES), lambda b, p, pt: (layer, 0, 0)), const((LANES, 2 * LANES)),
           const((nr, nr))],
        out_specs=pl.BlockSpec((None, nh, hd), lambda b, p, pt: (b, 0, 0)),
        scratch_shapes=[pltpu.VMEM((nr, LANES), F32), pltpu.VMEM((1, LANES), F32), pltpu.VMEM((nh, hd), F32)],
    )
    return pl.pallas_call(
        functools.partial(_dec_attn_body, page=page, npp=npp),
        grid_spec=grid_spec,
        out_shape=jax.ShapeDtypeStruct((nb, nh, hd), F32),
        compiler_params=_cparams(("parallel", "arbitrary"), 48),
        name="dec_attn",
    )(page_table, q.reshape(nb, nh, hd), *([cache_k] * npp), *([cache_v] * npp), P["bias_dec"], P["seg"],
      P["blk"])


_MATMUL_WEIGHTS = ("w_in", "conv_pw", "lru_wa", "lru_wx", "w_out", "w_gu", "w_down")
_ROW_PARAMS = ("norm1_g", "conv_dw_b", "conv_ln_g", "conv_ln_b", "lru_conv_b", "lru_ba", "lru_bx", "lru_lambda",
               "q_norm_g", "k_norm_g", "grp_norm_g", "norm2_g")


def _prepare(weights, page, tk, npp):
    P = dict(weights)
    for name in _MATMUL_WEIGHTS:
        P[name] = weights[name].astype(BF16)
    for name in _ROW_PARAMS:
        v = weights[name]
        P[name] = v.reshape(v.shape[0], 1, v.shape[1])
    bias = weights["sb_bias"]
    depth, nh = bias.shape
    P["bias_prompt"] = jnp.broadcast_to(bias.reshape(depth, nh, 1, 1), (depth, nh, 1, tk))
    P["bias_dec"] = jnp.tile(bias, (1, LANES // nh)).reshape(depth, 1, LANES)
    tri = jnp.tril(jnp.ones((tk, tk), F32), -1)
    P["tri"] = (-jnp.concatenate([tri, tri], axis=0)).astype(BF16)
    lane = jnp.arange(LANES)
    same = (lane[:, None] % nh) == (lane[None, :] % nh)
    later = same & ((lane[:, None] // nh) > (lane[None, :] // nh))
    P["seg"] = jnp.concatenate([later, same], axis=1).astype(BF16)
    rows = page * nh // LANES
    ri = jnp.arange(rows * npp)
    grp = ri // rows
    P["blk"] = (((grp[None, :] == grp[:, None]) & (ri[None, :] > ri[:, None]))
                | (grp[None, :] < grp[:, None])).astype(BF16)
    return P


def _prompt_layer(x, P, layer, bsz, seq):
    proj = _in_proj(x, P["norm1_g"], P["w_in"], layer, tm=min(1024, bsz * seq), tn=1024)
    proj3 = proj.reshape(bsz, seq, proj.shape[1])
    c, cc = _conv_mixer(proj3, P, layer, tt=512)
    r, lc, h = _lru_mixer(proj3, P, layer, tt=512)
    qb, kb, vb, k, v = _qkv_prep(proj3, P, layer, tt=512)
    a = _sb_attn(qb, kb, vb, P, layer, tq=min(ATTN_TQ, seq), tk=ATTN_TK)
    m = bsz * seq
    x1 = _out_proj(c.reshape(m, -1), r.reshape(m, -1), a.reshape(m, -1), x, P, layer, tm=512)
    x2 = _ffn(x1, P, layer, tm=512, tf=512)
    return x2, cc, lc, h.reshape(bsz, -1), k, v


def _sample_layer(x, P, st_cc, st_lc, st_h, cache_k, cache_v, page_table, layer):
    nb = x.shape[0]
    proj = _in_proj(x, P["norm1_g"], P["w_in"], layer, tm=nb, tn=1024)
    c, r, q, k, v, ccn, lcn, hn = _dec_mix(proj, st_cc, st_lc, st_h, P, layer)
    a = _dec_attn(q, cache_k, cache_v, page_table, P, layer, min(DEC_PAGES, page_table.shape[1]))
    x1 = _out_proj(c, r, a.reshape(nb, SB_WIDTH), x, P, layer, tm=nb)
    x2 = _ffn(x1, P, layer, tm=nb, tf=512)
    return x2, ccn, lcn, hn, k, v


def kernel(x_prompt, x_sample, cache_k, cache_v, page_table, state_lru_h, state_lru_conv, state_cconv,
           norm1_g, w_in, conv_dw_w, conv_dw_b, conv_ln_g, conv_ln_b, conv_pw, lru_conv_w, lru_conv_b,
           lru_wa, lru_ba, lru_wx, lru_bx, lru_lambda, q_norm_g, k_norm_g, sb_bias, grp_norm_g, w_out,
           norm2_g, w_gu, w_down):
    weights = dict(norm1_g=norm1_g, w_in=w_in, conv_dw_w=conv_dw_w, conv_dw_b=conv_dw_b, conv_ln_g=conv_ln_g,
                   conv_ln_b=conv_ln_b, conv_pw=conv_pw, lru_conv_w=lru_conv_w, lru_conv_b=lru_conv_b,
                   lru_wa=lru_wa, lru_ba=lru_ba, lru_wx=lru_wx, lru_bx=lru_bx, lru_lambda=lru_lambda,
                   q_norm_g=q_norm_g, k_norm_g=k_norm_g, sb_bias=sb_bias, grp_norm_g=grp_norm_g, w_out=w_out,
                   norm2_g=norm2_g, w_gu=w_gu, w_down=w_down)
    depth = w_in.shape[0]
    bp, seq, d = x_prompt.shape
    bs = x_sample.shape[0]
    P = _prepare(weights, cache_k.shape[2], ATTN_TK, min(DEC_PAGES, page_table.shape[1]))
    xp = x_prompt.reshape(bp * seq, d)
    xs = x_sample.reshape(bs, d)
    outs = [[] for _ in range(10)]
    for l in range(depth):
        xp, ccp, lcp, hp, kp, vp = _prompt_layer(xp, P, l, bp, seq)
        xs, ccs, lcs, hs, ksn, vsn = _sample_layer(xs, P, state_cconv, state_lru_conv, state_lru_h,
                                                   cache_k, cache_v, page_table, l)
        kv = lambda t: t.reshape(bs, -1, SB_HEADS, HEAD_DIM)
        for lst, val in zip(outs, (kp, vp, kv(ksn), kv(vsn), hp, hs, lcp, lcs, ccp, ccs)):
            lst.append(val)
    return (xp.reshape(bp, seq, d), xs.reshape(bs, 1, d)) + tuple(jnp.stack(o) for o in outs)
```

```python
import functools

import jax
import jax.numpy as jnp
from jax import lax
from jax.experimental import pallas as pl
from jax.experimental.pallas import tpu as pltpu

F32 = jnp.float32
BF16 = jnp.bfloat16

CONV_WIDTH = 512
CONV_K = 31
LRU_WIDTH = 512
LRU_BLOCKS = 4
LRU_BLOCK = LRU_WIDTH // LRU_BLOCKS
LRU_CONV_K = 4
LRU_C = 8.0
SB_HEADS = 8
HEAD_DIM = 128
SB_WIDTH = SB_HEADS * HEAD_DIM
EPS = 1e-6
LANES = 128
SUBLANES = 8
MIB = 1024 * 1024

COL_CV, COL_CG, COL_LX, COL_LG = 0, 1, 2, 3
COL_Q, COL_K, COL_V = 2, 3, 4

ATTN_TQ = 2048
ATTN_TK = 256
DEC_PAGES = 16


def _cparams(semantics, vmem_mib):
    return pltpu.CompilerParams(dimension_semantics=semantics, vmem_limit_bytes=vmem_mib * MIB)


def _layer_spec(tail, layer):
    return pl.BlockSpec((None,) + tuple(tail), lambda *_: (layer,) + (0,) * len(tail))


def _rms(v, g):
    return v * lax.rsqrt(jnp.mean(v * v, axis=-1, keepdims=True) + EPS) * g


def _dot(a, b):
    return jnp.dot(a, b, preferred_element_type=F32)


def _dot_nt(a, b):
    return lax.dot_general(a, b, (((1,), (1,)), ((), ())), preferred_element_type=F32)


def _softplus(z):
    return jnp.maximum(z, 0.0) + jnp.log(1.0 + jnp.exp(-jnp.abs(z)))


SOFTPLUS_LINEAR_ABOVE = 80.0


def _softplus_fast(z):
    return jnp.maximum(z, jnp.log(1.0 + jnp.exp(jnp.minimum(z, SOFTPLUS_LINEAR_ABOVE))))


def _gelu_tanh(x):
    return 0.5 * x * (1.0 + jnp.tanh(0.7978845608028654 * (x + 0.044715 * (x * x * x))))


def _split_bf16(x):
    hi = x.astype(BF16)
    return hi, (x - hi.astype(F32)).astype(BF16)


def _in_proj_body(x_ref, g_ref, w_ref, o_ref, xn_ref):
    @pl.when(pl.program_id(1) == 0)
    def _():
        xn_ref[...] = _rms(x_ref[...], g_ref[...]).astype(BF16)

    o_ref[...] = _dot(xn_ref[...], w_ref[...].astype(BF16))


def _in_proj(x, g, w, layer, tm, tn):
    m, d = x.shape
    n = w.shape[2]
    return pl.pallas_call(
        _in_proj_body,
        grid=(m // tm, n // tn),
        in_specs=[pl.BlockSpec((tm, d), lambda i, j: (i, 0)),
                  _layer_spec((1, d), layer),
                  pl.BlockSpec((None, d, tn), lambda i, j: (layer, 0, j))],
        out_specs=pl.BlockSpec((tm, tn), lambda i, j: (i, j)),
        out_shape=jax.ShapeDtypeStruct((m, n), F32),
        scratch_shapes=[pltpu.VMEM((tm, d), BF16)],
        compiler_params=_cparams(("parallel", "arbitrary"), 56),
        name="in_proj",
    )(x, g, w)


def _layernorm_silu(acc, g, b):
    mu = jnp.mean(acc, axis=-1, keepdims=True)
    xc = acc - mu
    var = jnp.mean(xc * xc, axis=-1, keepdims=True)
    y = xc * lax.rsqrt(var + EPS) * g + b
    return y * jax.nn.sigmoid(y)


def _conv_body(cv_ref, cg_ref, w_ref, b_ref, lng_ref, lnb_ref, pw_ref, c_ref, cc_ref, g_scr, *, tt, rc):
    halo = 32

    @pl.when(pl.program_id(1) == 0)
    def _():
        g_scr[0:halo, :] = jnp.zeros((halo, CONV_WIDTH), F32)

    g_scr[halo:halo + tt, :] = cv_ref[...] * jax.nn.sigmoid(cg_ref[...])
    first = halo - (CONV_K - 1)
    for r0 in range(0, tt, rc):
        acc = jnp.broadcast_to(b_ref[...], (rc, CONV_WIDTH))
        for s in range(SUBLANES):
            ext = rc + (SUBLANES if s else 0)
            part = None
            for j in range(CONV_K):
                off = first + j
                if off % SUBLANES != s:
                    continue
                base = r0 + off - s
                term = w_ref[j:j + 1, :] * g_scr[base:base + ext, :]
                part = term if part is None else part + term
            acc = acc + (part[s:s + rc, :] if s else part)
        y = _layernorm_silu(acc, lng_ref[...], lnb_ref[...])
        c_ref[r0:r0 + rc, :] = _dot(y.astype(BF16), pw_ref[...])
    cc_ref[...] = g_scr[tt + first:tt + halo, :]
    g_scr[0:halo, :] = g_scr[tt:tt + halo, :]


def _conv_mixer(proj3, P, layer, tt):
    b, t, _ = proj3.shape
    cw = CONV_WIDTH
    row = _layer_spec((1, cw), layer)
    return pl.pallas_call(
        functools.partial(_conv_body, tt=tt, rc=64),
        grid=(b, t // tt),
        in_specs=[pl.BlockSpec((None, tt, cw), lambda bi, ti: (bi, ti, COL_CV)),
                  pl.BlockSpec((None, tt, cw), lambda bi, ti: (bi, ti, COL_CG)),
                  _layer_spec((CONV_K, cw), layer), row, row, row, _layer_spec((cw, cw), layer)],
        out_specs=[pl.BlockSpec((None, tt, cw), lambda bi, ti: (bi, ti, 0)),
                   pl.BlockSpec((None, CONV_K - 1, cw), lambda bi, ti: (bi, 0, 0))],
        out_shape=[jax.ShapeDtypeStruct((b, t, cw), F32),
                   jax.ShapeDtypeStruct((b, CONV_K - 1, cw), F32)],
        scratch_shapes=[pltpu.VMEM((32 + tt, cw), F32)],
        compiler_params=_cparams(("parallel", "arbitrary"), 32),
        name="conv_mixer",
    )(proj3, proj3, P["conv_dw_w"], P["conv_dw_b"], P["conv_ln_g"], P["conv_ln_b"], P["conv_pw"])


def _lru_gates(xc, wa_ref, ba_ref, wx_ref, bx_ref, sp):
    a_parts, u_parts = [], []
    for n in range(LRU_BLOCKS):
        sl = slice(n * LRU_BLOCK, (n + 1) * LRU_BLOCK)
        xb = xc[:, sl]
        xbb = xb.astype(BF16)
        r = jax.nn.sigmoid(_dot(xbb, wa_ref[n]) + ba_ref[:, sl])
        i = jax.nn.sigmoid(_dot(xbb, wx_ref[n]) + bx_ref[:, sl])
        a = jnp.exp(-LRU_C * r * sp[:, sl])
        a_parts.append(a)
        u_parts.append(jnp.sqrt(1.0 - a * a) * (i * xb))
    return a_parts, u_parts


def _lru_body(lx_ref, lg_ref, cw_ref, cb_ref, wa_ref, ba_ref, wx_ref, bx_ref, lam_ref,
              r_ref, lc_ref, h_ref, x_scr, a_scr, u_scr, h_scr, *, tt):
    halo = 8

    @pl.when(pl.program_id(1) == 0)
    def _():
        x_scr[0:halo, :] = jnp.zeros((halo, LRU_WIDTH), F32)
        h_scr[...] = jnp.zeros((1, LRU_WIDTH), F32)

    x_scr[halo:halo + tt, :] = lx_ref[...]
    first = halo - (LRU_CONV_K - 1)
    xc = jnp.broadcast_to(cb_ref[...], (tt, LRU_WIDTH))
    for j in range(LRU_CONV_K):
        xc = xc + cw_ref[j:j + 1, :] * x_scr[first + j:first + j + tt, :]
    sp = _softplus(-lam_ref[...])
    a_parts, u_parts = _lru_gates(xc, wa_ref, ba_ref, wx_ref, bx_ref, sp)
    for n in range(LRU_BLOCKS):
        sl = slice(n * LRU_BLOCK, (n + 1) * LRU_BLOCK)
        a_scr[:, sl] = a_parts[n]
        u_scr[:, sl] = u_parts[n]

    def step(i, h):
        h = a_scr[pl.ds(i, 1), :] * h + u_scr[pl.ds(i, 1), :]
        u_scr[pl.ds(i, 1), :] = h
        return h

    h = lax.fori_loop(0, tt, step, h_scr[...], unroll=8)
    h_scr[...] = h
    h_ref[...] = h
    r_ref[...] = u_scr[...] * _gelu_tanh(lg_ref[...])
    lc_ref[...] = x_scr[tt + first:tt + halo, :]
    x_scr[0:halo, :] = x_scr[tt:tt + halo, :]


def _lru_mixer(proj3, P, layer, tt):
    b, t, _ = proj3.shape
    lw = LRU_WIDTH
    row = _layer_spec((1, lw), layer)
    gate = _layer_spec((LRU_BLOCKS, LRU_BLOCK, LRU_BLOCK), layer)
    return pl.pallas_call(
        functools.partial(_lru_body, tt=tt),
        grid=(b, t // tt),
        in_specs=[pl.BlockSpec((None, tt, lw), lambda bi, ti: (bi, ti, COL_LX)),
                  pl.BlockSpec((None, tt, lw), lambda bi, ti: (bi, ti, COL_LG)),
                  _layer_spec((LRU_CONV_K, lw), layer), row, gate, row, gate, row, row],
        out_specs=[pl.BlockSpec((None, tt, lw), lambda bi, ti: (bi, ti, 0)),
                   pl.BlockSpec((None, LRU_CONV_K - 1, lw), lambda bi, ti: (bi, 0, 0)),
                   pl.BlockSpec((None, 1, lw), lambda bi, ti: (bi, 0, 0))],
        out_shape=[jax.ShapeDtypeStruct((b, t, lw), F32),
                   jax.ShapeDtypeStruct((b, LRU_CONV_K - 1, lw), F32),
                   jax.ShapeDtypeStruct((b, 1, lw), F32)],
        scratch_shapes=[pltpu.VMEM((8 + tt, lw), F32), pltpu.VMEM((tt, lw), F32),
                        pltpu.VMEM((tt, lw), F32), pltpu.VMEM((1, lw), F32)],
        compiler_params=_cparams(("parallel", "arbitrary"), 32),
        name="lru_mixer",
    )(proj3, proj3, P["lru_conv_w"], P["lru_conv_b"], P["lru_wa"], P["lru_ba"], P["lru_wx"], P["lru_bx"],
      P["lru_lambda"])


def _qkv_body(q_ref, k_ref, v_ref, qg_ref, kg_ref, qb_ref, kb_ref, vb_ref, ko_ref, vo_ref):
    tt = q_ref.shape[0]
    kns = []
    for h in range(SB_HEADS):
        sl = slice(h * HEAD_DIM, (h + 1) * HEAD_DIM)
        qb_ref[:, sl] = _rms(q_ref[:, sl], qg_ref[...]).astype(BF16)
        kn = _rms(k_ref[:, sl], kg_ref[...])
        kb_ref[:, sl] = kn.astype(BF16)
        kns.append(kn)
    ko_ref[...] = jnp.concatenate(kns, axis=1).reshape(tt, SB_HEADS, HEAD_DIM)
    v = v_ref[...]
    vb_ref[...] = v.astype(BF16)
    vo_ref[...] = v.reshape(tt, SB_HEADS, HEAD_DIM)


def _qkv_prep(proj3, P, layer, tt):
    b, t, _ = proj3.shape
    w = SB_WIDTH
    col = lambda c: pl.BlockSpec((None, tt, w), lambda bi, ti: (bi, ti, c))
    o2 = pl.BlockSpec((None, tt, w), lambda bi, ti: (bi, ti, 0))
    o4 = pl.BlockSpec((None, tt, SB_HEADS, HEAD_DIM), lambda bi, ti: (bi, ti, 0, 0))
    sd = jax.ShapeDtypeStruct
    return pl.pallas_call(
        _qkv_body,
        grid=(b, t // tt),
        in_specs=[col(COL_Q), col(COL_K), col(COL_V), _layer_spec((1, HEAD_DIM), layer),
                  _layer_spec((1, HEAD_DIM), layer)],
        out_specs=[o2, o2, o2, o4, o4],
        out_shape=[sd((b, t, w), BF16)] * 3 + [sd((b, t, SB_HEADS, HEAD_DIM), F32)] * 2,
        compiler_params=_cparams(("parallel", "parallel"), 48),
        name="qkv_prep",
    )(proj3, proj3, proj3, P["q_norm_g"], P["k_norm_g"])


def _attn_body(q_ref, k_ref, v_ref, bias_ref, tri_ref, o_ref, run_scr, acc_scr, *, tq, tk):
    i = pl.program_id(2)
    nsub = tq // tk
    scale = HEAD_DIM ** -0.5
    run_scr[...] = jnp.zeros_like(run_scr)
    acc_scr[...] = jnp.zeros_like(acc_scr)

    def tile(r0, m, kstart, diagonal):
        s = _dot_nt(q_ref[r0:r0 + m, :], k_ref[pl.ds(kstart, tk), :])
        z = s * scale + bias_ref[...]
        sp = _softplus_fast(z)
        spm = sp
        if diagonal:
            keep = (lax.broadcasted_iota(jnp.int32, (m, tk), 1) < lax.broadcasted_iota(jnp.int32, (m, tk), 0))
            spm = jnp.where(keep, sp, 0.0)
        hi, lo = _split_bf16(spm)
        later = _dot(jnp.concatenate([hi, lo], axis=1), tri_ref[...])
        run = run_scr[r0:r0 + m, :]
        p = jnp.exp(z - sp + later - run)
        if diagonal:
            p = jnp.where(keep, p, 0.0)
        acc_scr[r0:r0 + m, :] += _dot(p.astype(BF16), v_ref[pl.ds(kstart, tk), :])
        run_scr[r0:r0 + m, :] = run + (spm[:, 0:1] - later[:, 0:1])

    for c in reversed(range(nsub)):
        tile(c * tk, tq - c * tk, pl.multiple_of(i * tq + c * tk, tk), True)

    def body(jj, carry):
        tile(0, tq, pl.multiple_of((i * nsub - 1 - jj) * tk, tk), False)
        return carry

    lax.fori_loop(0, i * nsub, body, 0)
    o_ref[...] = acc_scr[...]


def _sb_attn(qb, kb, vb, P, layer, tq, tk):
    b, t, w = qb.shape
    hd = HEAD_DIM
    seq = pl.BlockSpec((None, t, hd), lambda bi, h, i: (bi, 0, h))
    return pl.pallas_call(
        functools.partial(_attn_body, tq=tq, tk=tk),
        grid=(b, SB_HEADS, t // tq),
        in_specs=[pl.BlockSpec((None, tq, hd), lambda bi, h, i: (bi, i, h)), seq, seq,
                  pl.BlockSpec((None, None, 1, tk), lambda bi, h, i: (layer, h, 0, 0)),
                  pl.BlockSpec((2 * tk, tk), lambda bi, h, i: (0, 0))],
        out_specs=pl.BlockSpec((None, tq, hd), lambda bi, h, i: (bi, i, h)),
        out_shape=jax.ShapeDtypeStruct((b, t, w), F32),
        scratch_shapes=[pltpu.VMEM((tq, 1), F32), pltpu.VMEM((tq, hd), F32)],
        compiler_params=_cparams(("parallel", "parallel", "arbitrary"), 56),
        name="sb_attn",
    )(qb, kb, vb, P["bias_prompt"], P["tri"])


def _out_proj_body(c_ref, r_ref, a_ref, x_ref, g_ref, w_ref, o_ref):
    c0, c1, c2 = CONV_WIDTH, CONV_WIDTH + LRU_WIDTH, CONV_WIDTH + LRU_WIDTH + SB_WIDTH
    cn = _rms(c_ref[...], g_ref[:, 0:c0]).astype(BF16)
    rn = _rms(r_ref[...], g_ref[:, c0:c1]).astype(BF16)
    an = _rms(a_ref[...], g_ref[:, c1:c2]).astype(BF16)
    y = _dot(cn, w_ref[0:c0, :]) + _dot(rn, w_ref[c0:c1, :]) + _dot(an, w_ref[c1:c2, :])
    o_ref[...] = x_ref[...] + y


def _out_proj(c, r, a, x, P, layer, tm):
    m, d = x.shape
    dm = P["w_out"].shape[1]
    rows = lambda width: pl.BlockSpec((tm, width), lambda i: (i, 0))
    return pl.pallas_call(
        _out_proj_body,
        grid=(m // tm,),
        in_specs=[rows(CONV_WIDTH), rows(LRU_WIDTH), rows(SB_WIDTH), rows(d),
                  _layer_spec((1, dm), layer), _layer_spec((dm, d), layer)],
        out_specs=rows(d),
        out_shape=jax.ShapeDtypeStruct((m, d), F32),
        compiler_params=_cparams(("parallel",), 56),
        name="out_proj",
    )(c, r, a, x, P["grp_norm_g"], P["w_out"])


def _ffn_body(x_ref, g_ref, wg_ref, wu_ref, wd_ref, o_ref, xn_scr, acc_scr):
    f = pl.program_id(1)

    @pl.when(f == 0)
    def _():
        xn_scr[...] = _rms(x_ref[...], g_ref[...]).astype(BF16)
        acc_scr[...] = jnp.zeros_like(acc_scr)

    xn = xn_scr[...]
    gate = _dot(xn, wg_ref[...])
    up = _dot(xn, wu_ref[...])
    act = (gate * jax.nn.sigmoid(gate) * up).astype(BF16)
    acc_scr[...] += _dot(act, wd_ref[...])

    @pl.when(f == pl.num_programs(1) - 1)
    def _():
        o_ref[...] = x_ref[...] + acc_scr[...]


def _ffn(x, P, layer, tm, tf):
    m, d = x.shape
    dff = P["w_down"].shape[1]
    nf = dff // tf
    return pl.pallas_call(
        _ffn_body,
        grid=(m // tm, nf),
        in_specs=[pl.BlockSpec((tm, d), lambda i, f: (i, 0)),
                  _layer_spec((1, d), layer),
                  pl.BlockSpec((None, d, tf), lambda i, f: (layer, 0, f)),
                  pl.BlockSpec((None, d, tf), lambda i, f: (layer, 0, nf + f)),
                  pl.BlockSpec((None, tf, d), lambda i, f: (layer, f, 0))],
        out_specs=pl.BlockSpec((tm, d), lambda i, f: (i, 0)),
        out_shape=jax.ShapeDtypeStruct((m, d), F32),
        scratch_shapes=[pltpu.VMEM((tm, d), BF16), pltpu.VMEM((tm, d), F32)],
        compiler_params=_cparams(("parallel", "arbitrary"), 56),
        name="ffn",
    )(x, P["norm2_g"], P["w_gu"], P["w_gu"], P["w_down"])


def _dec_mix_body(pr_ref, cc_ref, lc_ref, h0_ref, dw_ref, dwb_ref, lng_ref, lnb_ref, pw_ref,
                  cw_ref, cb_ref, wa_ref, ba_ref, wx_ref, bx_ref, lam_ref, qg_ref, kg_ref,
                  c_ref, r_ref, q_ref, k_ref, v_ref, ccn_ref, lcn_ref, hn_ref, y_scr, x_scr, *, nb):
    cw, lw = CONV_WIDTH, LRU_WIDTH
    glu = pr_ref[:, 0:cw] * jax.nn.sigmoid(pr_ref[:, cw:2 * cw])
    lx = pr_ref[:, 2 * cw:2 * cw + lw]
    lg = pr_ref[:, 2 * cw + lw:2 * cw + 2 * lw]
    nc, nl = CONV_K - 1, LRU_CONV_K - 1
    for b in range(nb):
        y_scr[b:b + 1, :] = jnp.sum(cc_ref[b] * dw_ref[0:nc, :], axis=0, keepdims=True)
        ccn_ref[b, 0:nc - 1, :] = cc_ref[b, 1:nc, :]
        ccn_ref[b, nc - 1:nc, :] = glu[b:b + 1, :]
        x_scr[b:b + 1, :] = jnp.sum(lc_ref[b] * cw_ref[0:nl, :], axis=0, keepdims=True)
        lcn_ref[b, 0:nl - 1, :] = lc_ref[b, 1:nl, :]
        lcn_ref[b, nl - 1:nl, :] = lx[b:b + 1, :]
    acc = y_scr[...] + dw_ref[nc:nc + 1, :] * glu + dwb_ref[...]
    y = _layernorm_silu(acc, lng_ref[...], lnb_ref[...])
    c_ref[...] = _dot(y.astype(BF16), pw_ref[...])
    xc = x_scr[...] + cw_ref[nl:nl + 1, :] * lx + cb_ref[...]
    sp = _softplus(-lam_ref[...])
    a_parts, u_parts = _lru_gates(xc, wa_ref, ba_ref, wx_ref, bx_ref, sp)
    gl = _gelu_tanh(lg)
    for n in range(LRU_BLOCKS):
        sl = slice(n * LRU_BLOCK, (n + 1) * LRU_BLOCK)
        h = a_parts[n] * h0_ref[:, sl] + u_parts[n]
        hn_ref[:, sl] = h
        r_ref[:, sl] = h * gl[:, sl]
    base = 2 * cw + 2 * lw
    for hh in range(SB_HEADS):
        sl = slice(hh * HEAD_DIM, (hh + 1) * HEAD_DIM)
        q_ref[:, sl] = _rms(pr_ref[:, base + hh * HEAD_DIM:base + (hh + 1) * HEAD_DIM], qg_ref[...])
        k_ref[:, sl] = _rms(pr_ref[:, base + SB_WIDTH + hh * HEAD_DIM:base + SB_WIDTH + (hh + 1) * HEAD_DIM],
                            kg_ref[...])
    v_ref[...] = pr_ref[:, base + 2 * SB_WIDTH:base + 3 * SB_WIDTH]


def _dec_mix(pr, st_cc, st_lc, st_h, P, layer):
    nb, n_in = pr.shape
    cw, lw = CONV_WIDTH, LRU_WIDTH
    nc, nl = CONV_K - 1, LRU_CONV_K - 1
    sd = jax.ShapeDtypeStruct
    whole = lambda shape: pl.BlockSpec(shape, lambda i: (0,) * len(shape))
    rowc, rowl, rowh = _layer_spec((1, cw), layer), _layer_spec((1, lw), layer), _layer_spec((1, HEAD_DIM), layer)
    gate = _layer_spec((LRU_BLOCKS, LRU_BLOCK, LRU_BLOCK), layer)
    return pl.pallas_call(
        functools.partial(_dec_mix_body, nb=nb),
        grid=(1,),
        in_specs=[whole((nb, n_in)), _layer_spec((nb, nc, cw), layer), _layer_spec((nb, nl, lw), layer),
                  _layer_spec((nb, lw), layer),
                  _layer_spec((CONV_K, cw), layer), rowc, rowc, rowc, _layer_spec((cw, cw), layer),
                  _layer_spec((LRU_CONV_K, lw), layer), rowl, gate, rowl, gate, rowl, rowl, rowh, rowh],
        out_specs=[whole((nb, cw)), whole((nb, lw)), whole((nb, SB_WIDTH)), whole((nb, SB_WIDTH)),
                   whole((nb, SB_WIDTH)), whole((nb, nc, cw)), whole((nb, nl, lw)), whole((nb, lw))],
        out_shape=[sd((nb, cw), F32), sd((nb, lw), F32), sd((nb, SB_WIDTH), F32), sd((nb, SB_WIDTH), F32),
                   sd((nb, SB_WIDTH), F32), sd((nb, nc, cw), F32), sd((nb, nl, lw), F32), sd((nb, lw), F32)],
        scratch_shapes=[pltpu.VMEM((nb, cw), F32), pltpu.VMEM((nb, lw), F32)],
        compiler_params=_cparams(("arbitrary",), 32),
        name="dec_mix",
    )(pr, st_cc, st_lc, st_h, P["conv_dw_w"], P["conv_dw_b"], P["conv_ln_g"], P["conv_ln_b"], P["conv_pw"],
      P["lru_conv_w"], P["lru_conv_b"], P["lru_wa"], P["lru_ba"], P["lru_wx"], P["lru_bx"], P["lru_lambda"],
      P["q_norm_g"], P["k_norm_g"])


def _dec_attn_body(pt_ref, q_ref, *refs, page, npp):
    del pt_ref
    k_refs, v_refs = refs[0:npp], refs[npp:2 * npp]
    bias_ref, seg_ref, blk_ref, o_ref, zc_scr, run_scr, acc_scr = refs[2 * npp:]
    p = pl.program_id(1)
    nh, hd = SB_HEADS, HEAD_DIM
    nr = page * nh // LANES * npp

    @pl.when(p == 0)
    def _():
        run_scr[...] = jnp.zeros_like(run_scr)
        acc_scr[...] = jnp.zeros_like(acc_scr)

    own = (lax.broadcasted_iota(jnp.int32, (nh, LANES), 1) & (nh - 1)) == lax.broadcasted_iota(
        jnp.int32, (nh, LANES), 0)
    qb = q_ref[...].astype(BF16)
    kb = jnp.concatenate([k_refs[u][...].reshape(page * nh, hd).astype(BF16) for u in range(npp)], axis=0)
    z2 = _dot_nt(qb, kb)
    for r in range(nr):
        zc_scr[r:r + 1, :] = jnp.sum(jnp.where(own, z2[:, r * LANES:(r + 1) * LANES], 0.0), axis=0, keepdims=True)
    z = zc_scr[...] * (hd ** -0.5) + bias_ref[...]
    sp = _softplus(z)
    hi, lo = _split_bf16(sp)
    both = _dot(hi, seg_ref[...]) + _dot(lo, seg_ref[...])
    later_row, row_tot = both[:, 0:LANES], both[:, LANES:2 * LANES]
    thi, tlo = _split_bf16(row_tot)
    later_blk = _dot(blk_ref[...], thi) + _dot(blk_ref[...], tlo)
    w = jnp.exp(z - sp - later_row - later_blk - run_scr[...])
    a2 = jnp.concatenate(
        [jnp.where(own, jnp.broadcast_to(w[r:r + 1, :], (nh, LANES)), 0.0) for r in range(nr)], axis=1)
    vb = jnp.concatenate([v_refs[u][...].reshape(page * nh, hd).astype(BF16) for u in range(npp)], axis=0)
    acc_scr[...] += _dot(a2.astype(BF16), vb)
    run_scr[...] += jnp.sum(row_tot, axis=0, keepdims=True)

    @pl.when(p == pl.num_programs(1) - 1)
    def _():
        o_ref[...] = acc_scr[...]


def _dec_attn(q, cache_k, cache_v, page_table, P, layer, npp):
    nb = q.shape[0]
    n_pages = page_table.shape[1]
    page = cache_k.shape[2]
    nh, hd = SB_HEADS, HEAD_DIM
    nr = page * nh // LANES * npp

    def kv_spec(u):
        return pl.BlockSpec((None, None, page, nh, hd),
                            lambda b, p, pt: (layer, pt[b, n_pages - 1 - (p * npp + u)], 0, 0, 0))

    const = lambda shape: pl.BlockSpec(shape, lambda b, p, pt: (0,) * len(shape))
    grid_spec = pltpu.PrefetchScalarGridSpec(
        num_scalar_prefetch=1,
        grid=(nb, n_pages // npp),
        in_specs=[pl.BlockSpec((None, nh, hd), lambda b, p, pt: (b, 0, 0))]
        + [kv_spec(u) for u in range(npp)] + [kv_spec(u) for u in range(npp)]
        + [pl.BlockSpec((None, 1, LANES), lambda b, p, pt: (layer, 0, 0)), const((LANES, 2 * LANES)),
           const((nr, nr))],
        out_specs=pl.BlockSpec((None, nh, hd), lambda b, p, pt: (b, 0, 0)),
        scratch_shapes=[pltpu.VMEM((nr, LANES), F32), pltpu.VMEM((1, LANES), F32), pltpu.VMEM((nh, hd), F32)],
    )
    return pl.pallas_call(
        functools.partial(_dec_attn_body, page=page, npp=npp),
        grid_spec=grid_spec,
        out_shape=jax.ShapeDtypeStruct((nb, nh, hd), F32),
        compiler_params=_cparams(("parallel", "arbitrary"), 48),
        name="dec_attn",
    )(page_table, q.reshape(nb, nh, hd), *([cache_k] * npp), *([cache_v] * npp), P["bias_dec"], P["seg"],
      P["blk"])


_MATMUL_WEIGHTS = ("conv_pw", "lru_wa", "lru_wx", "w_out", "w_gu", "w_down")
_ROW_PARAMS = ("norm1_g", "conv_dw_b", "conv_ln_g", "conv_ln_b", "lru_conv_b", "lru_ba", "lru_bx", "lru_lambda",
               "q_norm_g", "k_norm_g", "grp_norm_g", "norm2_g")


def _prepare(weights, page, tk, npp):
    P = dict(weights)
    for name in _MATMUL_WEIGHTS:
        P[name] = weights[name].astype(BF16)
    for name in _ROW_PARAMS:
        v = weights[name]
        P[name] = v.reshape(v.shape[0], 1, v.shape[1])
    bias = weights["sb_bias"]
    depth, nh = bias.shape
    P["bias_prompt"] = jnp.broadcast_to(bias.reshape(depth, nh, 1, 1), (depth, nh, 1, tk))
    P["bias_dec"] = jnp.tile(bias, (1, LANES // nh)).reshape(depth, 1, LANES)
    tri = jnp.tril(jnp.ones((tk, tk), F32), -1)
    P["tri"] = (-jnp.concatenate([tri, tri], axis=0)).astype(BF16)
    lane = jnp.arange(LANES)
    same = (lane[:, None] % nh) == (lane[None, :] % nh)
    later = same & ((lane[:, None] // nh) > (lane[None, :] // nh))
    P["seg"] = jnp.concatenate([later, same], axis=1).astype(BF16)
    rows = page * nh // LANES
    ri = jnp.arange(rows * npp)
    grp = ri // rows
    P["blk"] = (((grp[None, :] == grp[:, None]) & (ri[None, :] > ri[:, None]))
                | (grp[None, :] < grp[:, None])).astype(BF16)
    return P


def _prompt_layer(x, P, layer, bsz, seq):
    proj = _in_proj(x, P["norm1_g"], P["w_in"], layer, tm=min(1024, bsz * seq), tn=1024)
    proj3 = proj.reshape(bsz, seq, proj.shape[1])
    c, cc = _conv_mixer(proj3, P, layer, tt=512)
    r, lc, h = _lru_mixer(proj3, P, layer, tt=512)
    qb, kb, vb, k, v = _qkv_prep(proj3, P, layer, tt=512)
    a = _sb_attn(qb, kb, vb, P, layer, tq=min(ATTN_TQ, seq), tk=ATTN_TK)
    m = bsz * seq
    x1 = _out_proj(c.reshape(m, -1), r.reshape(m, -1), a.reshape(m, -1), x, P, layer, tm=512)
    x2 = _ffn(x1, P, layer, tm=512, tf=512)
    return x2, cc, lc, h.reshape(bsz, -1), k, v


def _sample_layer(x, P, st_cc, st_lc, st_h, cache_k, cache_v, page_table, layer):
    nb = x.shape[0]
    proj = _in_proj(x, P["norm1_g"], P["w_in"], layer, tm=nb, tn=1024)
    c, r, q, k, v, ccn, lcn, hn = _dec_mix(proj, st_cc, st_lc, st_h, P, layer)
    a = _dec_attn(q, cache_k, cache_v, page_table, P, layer, min(DEC_PAGES, page_table.shape[1]))
    x1 = _out_proj(c, r, a.reshape(nb, SB_WIDTH), x, P, layer, tm=nb)
    x2 = _ffn(x1, P, layer, tm=nb, tf=512)
    return x2, ccn, lcn, hn, k, v


def kernel(x_prompt, x_sample, cache_k, cache_v, page_table, state_lru_h, state_lru_conv, state_cconv,
           norm1_g, w_in, conv_dw_w, conv_dw_b, conv_ln_g, conv_ln_b, conv_pw, lru_conv_w, lru_conv_b,
           lru_wa, lru_ba, lru_wx, lru_bx, lru_lambda, q_norm_g, k_norm_g, sb_bias, grp_norm_g, w_out,
           norm2_g, w_gu, w_down):
    weights = dict(norm1_g=norm1_g, w_in=w_in, conv_dw_w=conv_dw_w, conv_dw_b=conv_dw_b, conv_ln_g=conv_ln_g,
                   conv_ln_b=conv_ln_b, conv_pw=conv_pw, lru_conv_w=lru_conv_w, lru_conv_b=lru_conv_b,
                   lru_wa=lru_wa, lru_ba=lru_ba, lru_wx=lru_wx, lru_bx=lru_bx, lru_lambda=lru_lambda,
                   q_norm_g=q_norm_g, k_norm_g=k_norm_g, sb_bias=sb_bias, grp_norm_g=grp_norm_g, w_out=w_out,
                   norm2_g=norm2_g, w_gu=w_gu, w_down=w_down)
    depth = w_in.shape[0]
    bp, seq, d = x_prompt.shape
    bs = x_sample.shape[0]
    P = _prepare(weights, cache_k.shape[2], ATTN_TK, min(DEC_PAGES, page_table.shape[1]))
    xp = x_prompt.reshape(bp * seq, d)
    xs = x_sample.reshape(bs, d)
    outs = [[] for _ in range(10)]
    for l in range(depth):
        xp, ccp, lcp, hp, kp, vp = _prompt_layer(xp, P, l, bp, seq)
        xs, ccs, lcs, hs, ksn, vsn = _sample_layer(xs, P, state_cconv, state_lru_conv, state_lru_h,
                                                   cache_k, cache_v, page_table, l)
        kv = lambda t: t.reshape(bs, -1, SB_HEADS, HEAD_DIM)
        for lst, val in zip(outs, (kp, vp, kv(ksn), kv(vsn), hp, hs, lcp, lcs, ccp, ccs)):
            lst.append(val)
    return (xp.reshape(bp, seq, d), xs.reshape(bs, 1, d)) + tuple(jnp.stack(o) for o in outs)
```

```python
import functools

import jax
import jax.numpy as jnp
from jax import lax
from jax.experimental import pallas as pl
from jax.experimental.pallas import tpu as pltpu

F32 = jnp.float32
BF16 = jnp.bfloat16

CONV_WIDTH = 512
CONV_K = 31
LRU_WIDTH = 512
LRU_BLOCKS = 4
LRU_BLOCK = LRU_WIDTH // LRU_BLOCKS
LRU_CONV_K = 4
LRU_C = 8.0
SB_HEADS = 8
HEAD_DIM = 128
SB_WIDTH = SB_HEADS * HEAD_DIM
EPS = 1e-6
LANES = 128
SUBLANES = 8
MIB = 1024 * 1024

COL_CV, COL_CG, COL_LX, COL_LG = 0, 1, 2, 3
COL_Q, COL_K, COL_V = 2, 3, 4

ATTN_TQ = 2048
ATTN_TK = 256
DEC_PAGES = 16


def _cparams(semantics, vmem_mib):
    return pltpu.CompilerParams(dimension_semantics=semantics, vmem_limit_bytes=vmem_mib * MIB)


def _layer_spec(tail, layer):
    return pl.BlockSpec((None,) + tuple(tail), lambda *_: (layer,) + (0,) * len(tail))


def _rms(v, g):
    return v * lax.rsqrt(jnp.mean(v * v, axis=-1, keepdims=True) + EPS) * g


def _dot(a, b):
    return jnp.dot(a, b, preferred_element_type=F32)


def _dot_nt(a, b):
    return lax.dot_general(a, b, (((1,), (1,)), ((), ())), preferred_element_type=F32)


def _softplus(z):
    return jnp.maximum(z, 0.0) + jnp.log(1.0 + jnp.exp(-jnp.abs(z)))


SOFTPLUS_LINEAR_ABOVE = 80.0


def _softplus_fast(z):
    return jnp.maximum(z, jnp.log(1.0 + jnp.exp(jnp.minimum(z, SOFTPLUS_LINEAR_ABOVE))))


def _gelu_tanh(x):
    return 0.5 * x * (1.0 + jnp.tanh(0.7978845608028654 * (x + 0.044715 * (x * x * x))))


def _split_bf16(x):
    hi = x.astype(BF16)
    return hi, (x - hi.astype(F32)).astype(BF16)


def _in_proj_body(x_ref, g_ref, w_ref, o_ref, xn_ref):
    @pl.when(pl.program_id(1) == 0)
    def _():
        xn_ref[...] = _rms(x_ref[...], g_ref[...]).astype(BF16)

    o_ref[...] = _dot(xn_ref[...], w_ref[...])


def _in_proj(x, g, w, layer, tm, tn):
    m, d = x.shape
    n = w.shape[2]
    return pl.pallas_call(
        _in_proj_body,
        grid=(m // tm, n // tn),
        in_specs=[pl.BlockSpec((tm, d), lambda i, j: (i, 0)),
                  _layer_spec((1, d), layer),
                  pl.BlockSpec((None, d, tn), lambda i, j: (layer, 0, j))],
        out_specs=pl.BlockSpec((tm, tn), lambda i, j: (i, j)),
        out_shape=jax.ShapeDtypeStruct((m, n), F32),
        scratch_shapes=[pltpu.VMEM((tm, d), BF16)],
        compiler_params=_cparams(("parallel", "arbitrary"), 56),
        name="in_proj",
    )(x, g, w)


def _layernorm_silu(acc, g, b):
    mu = jnp.mean(acc, axis=-1, keepdims=True)
    xc = acc - mu
    var = jnp.mean(xc * xc, axis=-1, keepdims=True)
    y = xc * lax.rsqrt(var + EPS) * g + b
    return y * jax.nn.sigmoid(y)


def _conv_body(cv_ref, cg_ref, w_ref, b_ref, lng_ref, lnb_ref, pw_ref, c_ref, cc_ref, g_scr, *, tt, rc):
    halo = 32

    @pl.when(pl.program_id(1) == 0)
    def _():
        g_scr[0:halo, :] = jnp.zeros((halo, CONV_WIDTH), F32)

    g_scr[halo:halo + tt, :] = cv_ref[...] * jax.nn.sigmoid(cg_ref[...])
    first = halo - (CONV_K - 1)
    for r0 in range(0, tt, rc):
        acc = jnp.broadcast_to(b_ref[...], (rc, CONV_WIDTH))
        for s in range(SUBLANES):
            ext = rc + (SUBLANES if s else 0)
            part = None
            for j in range(CONV_K):
                off = first + j
                if off % SUBLANES != s:
                    continue
                base = r0 + off - s
                term = w_ref[j:j + 1, :] * g_scr[base:base + ext, :]
                part = term if part is None else part + term
            acc = acc + (part[s:s + rc, :] if s else part)
        y = _layernorm_silu(acc, lng_ref[...], lnb_ref[...])
        c_ref[r0:r0 + rc, :] = _dot(y.astype(BF16), pw_ref[...])
    cc_ref[...] = g_scr[tt + first:tt + halo, :]
    g_scr[0:halo, :] = g_scr[tt:tt + halo, :]


def _conv_mixer(proj3, P, layer, tt):
    b, t, _ = proj3.shape
    cw = CONV_WIDTH
    row = _layer_spec((1, cw), layer)
    return pl.pallas_call(
        functools.partial(_conv_body, tt=tt, rc=64),
        grid=(b, t // tt),
        in_specs=[pl.BlockSpec((None, tt, cw), lambda bi, ti: (bi, ti, COL_CV)),
                  pl.BlockSpec((None, tt, cw), lambda bi, ti: (bi, ti, COL_CG)),
                  _layer_spec((CONV_K, cw), layer), row, row, row, _layer_spec((cw, cw), layer)],
        out_specs=[pl.BlockSpec((None, tt, cw), lambda bi, ti: (bi, ti, 0)),
                   pl.BlockSpec((None, CONV_K - 1, cw), lambda bi, ti: (bi, 0, 0))],
        out_shape=[jax.ShapeDtypeStruct((b, t, cw), F32),
                   jax.ShapeDtypeStruct((b, CONV_K - 1, cw), F32)],
        scratch_shapes=[pltpu.VMEM((32 + tt, cw), F32)],
        compiler_params=_cparams(("parallel", "arbitrary"), 32),
        name="conv_mixer",
    )(proj3, proj3, P["conv_dw_w"], P["conv_dw_b"], P["conv_ln_g"], P["conv_ln_b"], P["conv_pw"])


def _lru_gates(xc, wa_ref, ba_ref, wx_ref, bx_ref, sp):
    a_parts, u_parts = [], []
    for n in range(LRU_BLOCKS):
        sl = slice(n * LRU_BLOCK, (n + 1) * LRU_BLOCK)
        xb = xc[:, sl]
        xbb = xb.astype(BF16)
        r = jax.nn.sigmoid(_dot(xbb, wa_ref[n]) + ba_ref[:, sl])
        i = jax.nn.sigmoid(_dot(xbb, wx_ref[n]) + bx_ref[:, sl])
        a = jnp.exp(-LRU_C * r * sp[:, sl])
        a_parts.append(a)
        u_parts.append(jnp.sqrt(1.0 - a * a) * (i * xb))
    return a_parts, u_parts


def _lru_body(lx_ref, lg_ref, cw_ref, cb_ref, wa_ref, ba_ref, wx_ref, bx_ref, lam_ref,
              r_ref, lc_ref, h_ref, x_scr, a_scr, u_scr, h_scr, *, tt):
    halo = 8

    @pl.when(pl.program_id(1) == 0)
    def _():
        x_scr[0:halo, :] = jnp.zeros((halo, LRU_WIDTH), F32)
        h_scr[...] = jnp.zeros((1, LRU_WIDTH), F32)

    x_scr[halo:halo + tt, :] = lx_ref[...]
    first = halo - (LRU_CONV_K - 1)
    xc = jnp.broadcast_to(cb_ref[...], (tt, LRU_WIDTH))
    for j in range(LRU_CONV_K):
        xc = xc + cw_ref[j:j + 1, :] * x_scr[first + j:first + j + tt, :]
    sp = _softplus(-lam_ref[...])
    a_parts, u_parts = _lru_gates(xc, wa_ref, ba_ref, wx_ref, bx_ref, sp)
    for n in range(LRU_BLOCKS):
        sl = slice(n * LRU_BLOCK, (n + 1) * LRU_BLOCK)
        a_scr[:, sl] = a_parts[n]
        u_scr[:, sl] = u_parts[n]

    def step(i, h):
        h = a_scr[pl.ds(i, 1), :] * h + u_scr[pl.ds(i, 1), :]
        u_scr[pl.ds(i, 1), :] = h
        return h

    h = lax.fori_loop(0, tt, step, h_scr[...], unroll=8)
    h_scr[...] = h
    h_ref[...] = h
    r_ref[...] = u_scr[...] * _gelu_tanh(lg_ref[...])
    lc_ref[...] = x_scr[tt + first:tt + halo, :]
    x_scr[0:halo, :] = x_scr[tt:tt + halo, :]


def _lru_mixer(proj3, P, layer, tt):
    b, t, _ = proj3.shape
    lw = LRU_WIDTH
    row = _layer_spec((1, lw), layer)
    gate = _layer_spec((LRU_BLOCKS, LRU_BLOCK, LRU_BLOCK), layer)
    return pl.pallas_call(
        functools.partial(_lru_body, tt=tt),
        grid=(b, t // tt),
        in_specs=[pl.BlockSpec((None, tt, lw), lambda bi, ti: (bi, ti, COL_LX)),
                  pl.BlockSpec((None, tt, lw), lambda bi, ti: (bi, ti, COL_LG)),
                  _layer_spec((LRU_CONV_K, lw), layer), row, gate, row, gate, row, row],
        out_specs=[pl.BlockSpec((None, tt, lw), lambda bi, ti: (bi, ti, 0)),
                   pl.BlockSpec((None, LRU_CONV_K - 1, lw), lambda bi, ti: (bi, 0, 0)),
                   pl.BlockSpec((None, 1, lw), lambda bi, ti: (bi, 0, 0))],
        out_shape=[jax.ShapeDtypeStruct((b, t, lw), F32),
                   jax.ShapeDtypeStruct((b, LRU_CONV_K - 1, lw), F32),
                   jax.ShapeDtypeStruct((b, 1, lw), F32)],
        scratch_shapes=[pltpu.VMEM((8 + tt, lw), F32), pltpu.VMEM((tt, lw), F32),
                        pltpu.VMEM((tt, lw), F32), pltpu.VMEM((1, lw), F32)],
        compiler_params=_cparams(("parallel", "arbitrary"), 32),
        name="lru_mixer",
    )(proj3, proj3, P["lru_conv_w"], P["lru_conv_b"], P["lru_wa"], P["lru_ba"], P["lru_wx"], P["lru_bx"],
      P["lru_lambda"])


def _qkv_body(q_ref, k_ref, v_ref, qg_ref, kg_ref, *rest, first):
    qb_ref, kb_ref, vb_ref, ko_ref, vo_ref = rest[-5:]
    tt = q_ref.shape[0]
    kns = []
    for h in range(SB_HEADS):
        sl = slice(h * HEAD_DIM, (h + 1) * HEAD_DIM)
        qb_ref[:, sl] = _rms(q_ref[:, sl], qg_ref[...]).astype(BF16)
        kn = _rms(k_ref[:, sl], kg_ref[...])
        kb_ref[:, sl] = kn.astype(BF16)
        kns.append(kn)
    k4 = jnp.concatenate(kns, axis=1).reshape(tt, SB_HEADS, HEAD_DIM)
    v = v_ref[...]
    vb_ref[...] = v.astype(BF16)
    v4 = v.reshape(tt, SB_HEADS, HEAD_DIM)
    if first:
        rest_shape = (ko_ref.shape[0] - 1,) + k4.shape
        ko_ref[0] = k4
        vo_ref[0] = v4
        ko_ref[1:] = jnp.zeros(rest_shape, F32)
        vo_ref[1:] = jnp.zeros(rest_shape, F32)
    else:
        ko_ref[...] = k4
        vo_ref[...] = v4


def _qkv_prep(proj3, P, layer, depth, k_all, v_all):
    b, t, _ = proj3.shape
    w = SB_WIDTH
    first = k_all is None
    tt = 256 if first else 512
    col = lambda c: pl.BlockSpec((None, tt, w), lambda bi, ti: (bi, ti, c))
    o2 = pl.BlockSpec((None, tt, w), lambda bi, ti: (bi, ti, 0))
    if first:
        o5 = pl.BlockSpec((depth, None, tt, SB_HEADS, HEAD_DIM), lambda bi, ti: (0, bi, ti, 0, 0))
    else:
        o5 = pl.BlockSpec((None, None, tt, SB_HEADS, HEAD_DIM), lambda bi, ti: (layer, bi, ti, 0, 0))
    sd = jax.ShapeDtypeStruct
    in_specs = [col(COL_Q), col(COL_K), col(COL_V), _layer_spec((1, HEAD_DIM), layer),
                _layer_spec((1, HEAD_DIM), layer)]
    args = [proj3, proj3, proj3, P["q_norm_g"], P["k_norm_g"]]
    aliases = {}
    if not first:
        in_specs += [pl.BlockSpec(memory_space=pl.ANY)] * 2
        args += [k_all, v_all]
        aliases = {5: 3, 6: 4}
    return pl.pallas_call(
        functools.partial(_qkv_body, first=first),
        grid=(b, t // tt),
        in_specs=in_specs,
        out_specs=[o2, o2, o2, o5, o5],
        out_shape=[sd((b, t, w), BF16)] * 3 + [sd((depth, b, t, SB_HEADS, HEAD_DIM), F32)] * 2,
        input_output_aliases=aliases,
        compiler_params=_cparams(("parallel", "parallel"), 48),
        name="qkv_prep",
    )(*args)


def _attn_body(q_ref, k_ref, v_ref, bias_ref, tri_ref, o_ref, run_scr, acc_scr, *, tq, tk):
    i = pl.program_id(2)
    nsub = tq // tk
    scale = HEAD_DIM ** -0.5
    run_scr[...] = jnp.zeros_like(run_scr)
    acc_scr[...] = jnp.zeros_like(acc_scr)

    def tile(r0, m, kstart, diagonal):
        s = _dot_nt(q_ref[r0:r0 + m, :], k_ref[pl.ds(kstart, tk), :])
        z = s * scale + bias_ref[...]
        sp = _softplus_fast(z)
        spm = sp
        if diagonal:
            keep = (lax.broadcasted_iota(jnp.int32, (m, tk), 1) < lax.broadcasted_iota(jnp.int32, (m, tk), 0))
            spm = jnp.where(keep, sp, 0.0)
        hi, lo = _split_bf16(spm)
        later = _dot(jnp.concatenate([hi, lo], axis=1), tri_ref[...])
        run = run_scr[r0:r0 + m, :]
        p = jnp.exp(z - sp + later - run)
        if diagonal:
            p = jnp.where(keep, p, 0.0)
        acc_scr[r0:r0 + m, :] += _dot(p.astype(BF16), v_ref[pl.ds(kstart, tk), :])
        run_scr[r0:r0 + m, :] = run + (spm[:, 0:1] - later[:, 0:1])

    for c in reversed(range(nsub)):
        tile(c * tk, tq - c * tk, pl.multiple_of(i * tq + c * tk, tk), True)

    def body(jj, carry):
        tile(0, tq, pl.multiple_of((i * nsub - 1 - jj) * tk, tk), False)
        return carry

    lax.fori_loop(0, i * nsub, body, 0)
    o_ref[...] = acc_scr[...]


def _sb_attn(qb, kb, vb, P, layer, tq, tk):
    b, t, w = qb.shape
    hd = HEAD_DIM
    seq = pl.BlockSpec((None, t, hd), lambda bi, h, i: (bi, 0, h))
    return pl.pallas_call(
        functools.partial(_attn_body, tq=tq, tk=tk),
        grid=(b, SB_HEADS, t // tq),
        in_specs=[pl.BlockSpec((None, tq, hd), lambda bi, h, i: (bi, i, h)), seq, seq,
                  pl.BlockSpec((None, None, 1, tk), lambda bi, h, i: (layer, h, 0, 0)),
                  pl.BlockSpec((2 * tk, tk), lambda bi, h, i: (0, 0))],
        out_specs=pl.BlockSpec((None, tq, hd), lambda bi, h, i: (bi, i, h)),
        out_shape=jax.ShapeDtypeStruct((b, t, w), F32),
        scratch_shapes=[pltpu.VMEM((tq, 1), F32), pltpu.VMEM((tq, hd), F32)],
        compiler_params=_cparams(("parallel", "parallel", "arbitrary"), 56),
        name="sb_attn",
    )(qb, kb, vb, P["bias_prompt"], P["tri"])


def _out_proj_body(c_ref, r_ref, a_ref, x_ref, g_ref, w_ref, o_ref):
    c0, c1, c2 = CONV_WIDTH, CONV_WIDTH + LRU_WIDTH, CONV_WIDTH + LRU_WIDTH + SB_WIDTH
    cn = _rms(c_ref[...], g_ref[:, 0:c0]).astype(BF16)
    rn = _rms(r_ref[...], g_ref[:, c0:c1]).astype(BF16)
    an = _rms(a_ref[...], g_ref[:, c1:c2]).astype(BF16)
    y = _dot(cn, w_ref[0:c0, :]) + _dot(rn, w_ref[c0:c1, :]) + _dot(an, w_ref[c1:c2, :])
    o_ref[...] = x_ref[...] + y


def _out_proj(c, r, a, x, P, layer, tm):
    m, d = x.shape
    dm = P["w_out"].shape[1]
    rows = lambda width: pl.BlockSpec((tm, width), lambda i: (i, 0))
    return pl.pallas_call(
        _out_proj_body,
        grid=(m // tm,),
        in_specs=[rows(CONV_WIDTH), rows(LRU_WIDTH), rows(SB_WIDTH), rows(d),
                  _layer_spec((1, dm), layer), _layer_spec((dm, d), layer)],
        out_specs=rows(d),
        out_shape=jax.ShapeDtypeStruct((m, d), F32),
        compiler_params=_cparams(("parallel",), 56),
        name="out_proj",
    )(c, r, a, x, P["grp_norm_g"], P["w_out"])


def _ffn_body(x_ref, g_ref, wg_ref, wu_ref, wd_ref, o_ref, xn_scr, acc_scr):
    f = pl.program_id(1)

    @pl.when(f == 0)
    def _():
        xn_scr[...] = _rms(x_ref[...], g_ref[...]).astype(BF16)
        acc_scr[...] = jnp.zeros_like(acc_scr)

    xn = xn_scr[...]
    gate = _dot(xn, wg_ref[...])
    up = _dot(xn, wu_ref[...])
    act = (gate * jax.nn.sigmoid(gate) * up).astype(BF16)
    acc_scr[...] += _dot(act, wd_ref[...])

    @pl.when(f == pl.num_programs(1) - 1)
    def _():
        o_ref[...] = x_ref[...] + acc_scr[...]


def _ffn(x, P, layer, tm, tf):
    m, d = x.shape
    dff = P["w_down"].shape[1]
    nf = dff // tf
    return pl.pallas_call(
        _ffn_body,
        grid=(m // tm, nf),
        in_specs=[pl.BlockSpec((tm, d), lambda i, f: (i, 0)),
                  _layer_spec((1, d), layer),
                  pl.BlockSpec((None, d, tf), lambda i, f: (layer, 0, f)),
                  pl.BlockSpec((None, d, tf), lambda i, f: (layer, 0, nf + f)),
                  pl.BlockSpec((None, tf, d), lambda i, f: (layer, f, 0))],
        out_specs=pl.BlockSpec((tm, d), lambda i, f: (i, 0)),
        out_shape=jax.ShapeDtypeStruct((m, d), F32),
        scratch_shapes=[pltpu.VMEM((tm, d), BF16), pltpu.VMEM((tm, d), F32)],
        compiler_params=_cparams(("parallel", "arbitrary"), 56),
        name="ffn",
    )(x, P["norm2_g"], P["w_gu"], P["w_gu"], P["w_down"])


def _dec_mix_body(pr_ref, cc_ref, lc_ref, h0_ref, dw_ref, dwb_ref, lng_ref, lnb_ref, pw_ref,
                  cw_ref, cb_ref, wa_ref, ba_ref, wx_ref, bx_ref, lam_ref, qg_ref, kg_ref,
                  c_ref, r_ref, q_ref, k_ref, v_ref, ccn_ref, lcn_ref, hn_ref, y_scr, x_scr, *, nb):
    cw, lw = CONV_WIDTH, LRU_WIDTH
    glu = pr_ref[:, 0:cw] * jax.nn.sigmoid(pr_ref[:, cw:2 * cw])
    lx = pr_ref[:, 2 * cw:2 * cw + lw]
    lg = pr_ref[:, 2 * cw + lw:2 * cw + 2 * lw]
    nc, nl = CONV_K - 1, LRU_CONV_K - 1
    for b in range(nb):
        y_scr[b:b + 1, :] = jnp.sum(cc_ref[b] * dw_ref[0:nc, :], axis=0, keepdims=True)
        ccn_ref[b, 0:nc - 1, :] = cc_ref[b, 1:nc, :]
        ccn_ref[b, nc - 1:nc, :] = glu[b:b + 1, :]
        x_scr[b:b + 1, :] = jnp.sum(lc_ref[b] * cw_ref[0:nl, :], axis=0, keepdims=True)
        lcn_ref[b, 0:nl - 1, :] = lc_ref[b, 1:nl, :]
        lcn_ref[b, nl - 1:nl, :] = lx[b:b + 1, :]
    acc = y_scr[...] + dw_ref[nc:nc + 1, :] * glu + dwb_ref[...]
    y = _layernorm_silu(acc, lng_ref[...], lnb_ref[...])
    c_ref[...] = _dot(y.astype(BF16), pw_ref[...])
    xc = x_scr[...] + cw_ref[nl:nl + 1, :] * lx + cb_ref[...]
    sp = _softplus(-lam_ref[...])
    a_parts, u_parts = _lru_gates(xc, wa_ref, ba_ref, wx_ref, bx_ref, sp)
    gl = _gelu_tanh(lg)
    for n in range(LRU_BLOCKS):
        sl = slice(n * LRU_BLOCK, (n + 1) * LRU_BLOCK)
        h = a_parts[n] * h0_ref[:, sl] + u_parts[n]
        hn_ref[:, sl] = h
        r_ref[:, sl] = h * gl[:, sl]
    base = 2 * cw + 2 * lw
    for hh in range(SB_HEADS):
        sl = slice(hh * HEAD_DIM, (hh + 1) * HEAD_DIM)
        q_ref[:, sl] = _rms(pr_ref[:, base + hh * HEAD_DIM:base + (hh + 1) * HEAD_DIM], qg_ref[...])
        k_ref[:, sl] = _rms(pr_ref[:, base + SB_WIDTH + hh * HEAD_DIM:base + SB_WIDTH + (hh + 1) * HEAD_DIM],
                            kg_ref[...])
    v_ref[...] = pr_ref[:, base + 2 * SB_WIDTH:base + 3 * SB_WIDTH]


def _dec_mix(pr, st_cc, st_lc, st_h, P, layer):
    nb, n_in = pr.shape
    cw, lw = CONV_WIDTH, LRU_WIDTH
    nc, nl = CONV_K - 1, LRU_CONV_K - 1
    sd = jax.ShapeDtypeStruct
    whole = lambda shape: pl.BlockSpec(shape, lambda i: (0,) * len(shape))
    rowc, rowl, rowh = _layer_spec((1, cw), layer), _layer_spec((1, lw), layer), _layer_spec((1, HEAD_DIM), layer)
    gate = _layer_spec((LRU_BLOCKS, LRU_BLOCK, LRU_BLOCK), layer)
    return pl.pallas_call(
        functools.partial(_dec_mix_body, nb=nb),
        grid=(1,),
        in_specs=[whole((nb, n_in)), _layer_spec((nb, nc, cw), layer), _layer_spec((nb, nl, lw), layer),
                  _layer_spec((nb, lw), layer),
                  _layer_spec((CONV_K, cw), layer), rowc, rowc, rowc, _layer_spec((cw, cw), layer),
                  _layer_spec((LRU_CONV_K, lw), layer), rowl, gate, rowl, gate, rowl, rowl, rowh, rowh],
        out_specs=[whole((nb, cw)), whole((nb, lw)), whole((nb, SB_WIDTH)), whole((nb, SB_WIDTH)),
                   whole((nb, SB_WIDTH)), whole((nb, nc, cw)), whole((nb, nl, lw)), whole((nb, lw))],
        out_shape=[sd((nb, cw), F32), sd((nb, lw), F32), sd((nb, SB_WIDTH), F32), sd((nb, SB_WIDTH), F32),
                   sd((nb, SB_WIDTH), F32), sd((nb, nc, cw), F32), sd((nb, nl, lw), F32), sd((nb, lw), F32)],
        scratch_shapes=[pltpu.VMEM((nb, cw), F32), pltpu.VMEM((nb, lw), F32)],
        compiler_params=_cparams(("arbitrary",), 32),
        name="dec_mix",
    )(pr, st_cc, st_lc, st_h, P["conv_dw_w"], P["conv_dw_b"], P["conv_ln_g"], P["conv_ln_b"], P["conv_pw"],
      P["lru_conv_w"], P["lru_conv_b"], P["lru_wa"], P["lru_ba"], P["lru_wx"], P["lru_bx"], P["lru_lambda"],
      P["q_norm_g"], P["k_norm_g"])


def _dec_attn_body(pt_ref, q_ref, *refs, page, npp):
    del pt_ref
    k_refs, v_refs = refs[0:npp], refs[npp:2 * npp]
    bias_ref, seg_ref, blk_ref, o_ref, zc_scr, run_scr, acc_scr = refs[2 * npp:]
    p = pl.program_id(1)
    nh, hd = SB_HEADS, HEAD_DIM
    nr = page * nh // LANES * npp

    @pl.when(p == 0)
    def _():
        run_scr[...] = jnp.zeros_like(run_scr)
        acc_scr[...] = jnp.zeros_like(acc_scr)

    own = (lax.broadcasted_iota(jnp.int32, (nh, LANES), 1) & (nh - 1)) == lax.broadcasted_iota(
        jnp.int32, (nh, LANES), 0)
    qb = q_ref[...].astype(BF16)
    kb = jnp.concatenate([k_refs[u][...].reshape(page * nh, hd).astype(BF16) for u in range(npp)], axis=0)
    z2 = _dot_nt(qb, kb)
    for r in range(nr):
        zc_scr[r:r + 1, :] = jnp.sum(jnp.where(own, z2[:, r * LANES:(r + 1) * LANES], 0.0), axis=0, keepdims=True)
    z = zc_scr[...] * (hd ** -0.5) + bias_ref[...]
    sp = _softplus(z)
    hi, lo = _split_bf16(sp)
    both = _dot(hi, seg_ref[...]) + _dot(lo, seg_ref[...])
    later_row, row_tot = both[:, 0:LANES], both[:, LANES:2 * LANES]
    thi, tlo = _split_bf16(row_tot)
    later_blk = _dot(blk_ref[...], thi) + _dot(blk_ref[...], tlo)
    w = jnp.exp(z - sp - later_row - later_blk - run_scr[...])
    a2 = jnp.concatenate(
        [jnp.where(own, jnp.broadcast_to(w[r:r + 1, :], (nh, LANES)), 0.0) for r in range(nr)], axis=1)
    vb = jnp.concatenate([v_refs[u][...].reshape(page * nh, hd).astype(BF16) for u in range(npp)], axis=0)
    acc_scr[...] += _dot(a2.astype(BF16), vb)
    run_scr[...] += jnp.sum(row_tot, axis=0, keepdims=True)

    @pl.when(p == pl.num_programs(1) - 1)
    def _():
        o_ref[...] = acc_scr[...]


def _dec_attn(q, cache_k, cache_v, page_table, P, layer, npp):
    nb = q.shape[0]
    n_pages = page_table.shape[1]
    page = cache_k.shape[2]
    nh, hd = SB_HEADS, HEAD_DIM
    nr = page * nh // LANES * npp

    def kv_spec(u):
        return pl.BlockSpec((None, None, page, nh, hd),
                            lambda b, p, pt: (layer, pt[b, n_pages - 1 - (p * npp + u)], 0, 0, 0))

    const = lambda shape: pl.BlockSpec(shape, lambda b, p, pt: (0,) * len(shape))
    grid_spec = pltpu.PrefetchScalarGridSpec(
        num_scalar_prefetch=1,
        grid=(nb, n_pages // npp),
        in_specs=[pl.BlockSpec((None, nh, hd), lambda b, p, pt: (b, 0, 0))]
        + [kv_spec(u) for u in range(npp)] + [kv_spec(u) for u in range(npp)]
        + [pl.BlockSpec((None, 1, LANES), lambda b, p, pt: (layer, 0, 0)), const((LANES, 2 * LANES)),
           const((nr, nr))],
        out_specs=pl.BlockSpec((None, nh, hd), lambda b, p, pt: (b, 0, 0)),
        scratch_shapes=[pltpu.VMEM((nr, LANES), F32), pltpu.VMEM((1, LANES), F32), pltpu.VMEM((nh, hd), F32)],
    )
    return pl.pallas_call(
        functools.partial(_dec_attn_body, page=page, npp=npp),
        grid_spec=grid_spec,
        out_shape=jax.ShapeDtypeStruct((nb, nh, hd), F32),
        compiler_params=_cparams(("parallel", "arbitrary"), 48),
        name="dec_attn",
    )(page_table, q.reshape(nb, nh, hd), *([cache_k] * npp), *([cache_v] * npp), P["bias_dec"], P["seg"],
      P["blk"])


_MATMUL_WEIGHTS = ("w_in", "conv_pw", "lru_wa", "lru_wx", "w_out", "w_gu", "w_down")
_ROW_PARAMS = ("norm1_g", "conv_dw_b", "conv_ln_g", "conv_ln_b", "lru_conv_b", "lru_ba", "lru_bx", "lru_lambda",
               "q_norm_g", "k_norm_g", "grp_norm_g", "norm2_g")


def _prepare(weights, page, tk, npp):
    P = dict(weights)
    for name in _MATMUL_WEIGHTS:
        P[name] = weights[name].astype(BF16)
    for name in _ROW_PARAMS:
        v = weights[name]
        P[name] = v.reshape(v.shape[0], 1, v.shape[1])
    bias = weights["sb_bias"]
    depth, nh = bias.shape
    P["bias_prompt"] = jnp.broadcast_to(bias.reshape(depth, nh, 1, 1), (depth, nh, 1, tk))
    P["bias_dec"] = jnp.tile(bias, (1, LANES // nh)).reshape(depth, 1, LANES)
    tri = jnp.tril(jnp.ones((tk, tk), F32), -1)
    P["tri"] = (-jnp.concatenate([tri, tri], axis=0)).astype(BF16)
    lane = jnp.arange(LANES)
    same = (lane[:, None] % nh) == (lane[None, :] % nh)
    later = same & ((lane[:, None] // nh) > (lane[None, :] // nh))
    P["seg"] = jnp.concatenate([later, same], axis=1).astype(BF16)
    rows = page * nh // LANES
    ri = jnp.arange(rows * npp)
    grp = ri // rows
    P["blk"] = (((grp[None, :] == grp[:, None]) & (ri[None, :] > ri[:, None]))
                | (grp[None, :] < grp[:, None])).astype(BF16)
    return P


def _prompt_layer(x, P, layer, depth, bsz, seq, k_all, v_all):
    proj = _in_proj(x, P["norm1_g"], P["w_in"], layer, tm=min(1024, bsz * seq), tn=1024)
    proj3 = proj.reshape(bsz, seq, proj.shape[1])
    c, cc = _conv_mixer(proj3, P, layer, tt=512)
    r, lc, h = _lru_mixer(proj3, P, layer, tt=512)
    qb, kb, vb, k_all, v_all = _qkv_prep(proj3, P, layer, depth, k_all, v_all)
    a = _sb_attn(qb, kb, vb, P, layer, tq=min(ATTN_TQ, seq), tk=ATTN_TK)
    m = bsz * seq
    x1 = _out_proj(c.reshape(m, -1), r.reshape(m, -1), a.reshape(m, -1), x, P, layer, tm=512)
    x2 = _ffn(x1, P, layer, tm=512, tf=512)
    return x2, cc, lc, h.reshape(bsz, -1), k_all, v_all


def _sample_layer(x, P, st_cc, st_lc, st_h, cache_k, cache_v, page_table, layer):
    nb = x.shape[0]
    proj = _in_proj(x, P["norm1_g"], P["w_in"], layer, tm=nb, tn=1024)
    c, r, q, k, v, ccn, lcn, hn = _dec_mix(proj, st_cc, st_lc, st_h, P, layer)
    a = _dec_attn(q, cache_k, cache_v, page_table, P, layer, min(DEC_PAGES, page_table.shape[1]))
    x1 = _out_proj(c, r, a.reshape(nb, SB_WIDTH), x, P, layer, tm=nb)
    x2 = _ffn(x1, P, layer, tm=nb, tf=512)
    return x2, ccn, lcn, hn, k, v


def kernel(x_prompt, x_sample, cache_k, cache_v, page_table, state_lru_h, state_lru_conv, state_cconv,
           norm1_g, w_in, conv_dw_w, conv_dw_b, conv_ln_g, conv_ln_b, conv_pw, lru_conv_w, lru_conv_b,
           lru_wa, lru_ba, lru_wx, lru_bx, lru_lambda, q_norm_g, k_norm_g, sb_bias, grp_norm_g, w_out,
           norm2_g, w_gu, w_down):
    weights = dict(norm1_g=norm1_g, w_in=w_in, conv_dw_w=conv_dw_w, conv_dw_b=conv_dw_b, conv_ln_g=conv_ln_g,
                   conv_ln_b=conv_ln_b, conv_pw=conv_pw, lru_conv_w=lru_conv_w, lru_conv_b=lru_conv_b,
                   lru_wa=lru_wa, lru_ba=lru_ba, lru_wx=lru_wx, lru_bx=lru_bx, lru_lambda=lru_lambda,
                   q_norm_g=q_norm_g, k_norm_g=k_norm_g, sb_bias=sb_bias, grp_norm_g=grp_norm_g, w_out=w_out,
                   norm2_g=norm2_g, w_gu=w_gu, w_down=w_down)
    depth = w_in.shape[0]
    bp, seq, d = x_prompt.shape
    bs = x_sample.shape[0]
    P = _prepare(weights, cache_k.shape[2], ATTN_TK, min(DEC_PAGES, page_table.shape[1]))
    xp = x_prompt.reshape(bp * seq, d)
    xs = x_sample.reshape(bs, d)
    outs = [[] for _ in range(8)]
    k_all = v_all = None
    for l in range(depth):
        xp, ccp, lcp, hp, k_all, v_all = _prompt_layer(xp, P, l, depth, bp, seq, k_all, v_all)
        xs, ccs, lcs, hs, ksn, vsn = _sample_layer(xs, P, state_cconv, state_lru_conv, state_lru_h,
                                                   cache_k, cache_v, page_table, l)
        kv = lambda t: t.reshape(bs, -1, SB_HEADS, HEAD_DIM)
        for lst, val in zip(outs, (kv(ksn), kv(vsn), hp, hs, lcp, lcs, ccp, ccs)):
            lst.append(val)
    return (xp.reshape(bp, seq, d), xs.reshape(bs, 1, d), k_all, v_all) + tuple(jnp.stack(o) for o in outs)
```
